```python
import jax, jax.numpy as jnp
from jax import lax
import numpy as np

D_MODEL = 2048
BATCH = 2
SEQ = 4096
DEPTH = 4
DEC_BATCH = 8
DEC_SEQ = 8
PAST_LEN = 16384
PAGE_SIZE = 128

HEAD_DIM = 128
MIX_WIDTH = D_MODEL
MEM_HEADS = 4
N_MEM = 256
MEM_WIDTH = MEM_HEADS * HEAD_DIM
TOK_WIDTH = MIX_WIDTH - MEM_WIDTH
NSA_HEADS = TOK_WIDTH // HEAD_DIM
NSA_KV = 4
CMP_BLOCK = 32
SLC_BLOCK = 64
TOP_N = 16
WINDOW = 512
FORCE_BONUS = 1.0e4
NSA_IN = NSA_HEADS * HEAD_DIM + 6 * NSA_KV * HEAD_DIM + 3 * NSA_HEADS + MEM_WIDTH
RET_HEAD_DIM = 256
RET_HEADS = TOK_WIDTH // RET_HEAD_DIM
RET_CHUNK = 128
ROPE_BASE = 10000.0
RET_IN = 4 * TOK_WIDTH + MEM_WIDTH
FFN_DIM = 5632
CONV_W = 3
NORM_EPS = 1e-6
Q_BLOCK = 128

kernel_name = 'nsa_retention_hybrid_step'


def rms_norm(x, g):
    xf = x.astype(jnp.float32)
    y = xf * lax.rsqrt(jnp.mean(xf * xf, axis=-1, keepdims=True) + NORM_EPS)
    return (y * g.astype(jnp.float32)).astype(x.dtype)


def masked_softmax(s, mask):
    s = jnp.where(mask, s.astype(jnp.float32), -jnp.inf)
    m = jnp.max(s, axis=-1, keepdims=True)
    m = jnp.where(jnp.isfinite(m), m, 0.0)
    e = jnp.where(mask, jnp.exp(s - m), 0.0)
    return e / jnp.maximum(jnp.sum(e, axis=-1, keepdims=True), 1e-30)


def split_cols(p, sizes):
    return jnp.split(p, np.cumsum(sizes)[:-1].tolist(), axis=-1)


def compress_blocks(rows, pe, w1, w2):
    b, l, g, d = rows.shape
    nc = l // CMP_BLOCK
    blk = rows[:, :nc * CMP_BLOCK].reshape(b, nc, CMP_BLOCK, g, d) + pe[None, None, :, None, :]
    flat = blk.transpose(0, 1, 3, 2, 4).reshape(b, nc, g, CMP_BLOCK * d)
    return jax.nn.gelu(flat @ w1) @ w2


def nsa_project(h, w_in):
    b, t, _ = h.shape
    kvw = NSA_KV * HEAD_DIM
    q, kc, vc, ks, vs, kw, vw, gates, qm = split_cols(h @ w_in, [NSA_HEADS * HEAD_DIM] + [kvw] * 6 + [3 * NSA_HEADS, MEM_WIDTH])
    kvs = (b, t, NSA_KV, HEAD_DIM)
    rows = jnp.stack([kc.reshape(kvs), vc.reshape(kvs), ks.reshape(kvs), vs.reshape(kvs)], axis=2)
    win = jnp.stack([kw.reshape(kvs), vw.reshape(kvs)], axis=2)
    return (q.reshape(b, t, NSA_HEADS, HEAD_DIM), rows, win,
            gates.reshape(b, t, NSA_HEADS, 3), qm.reshape(b, t, MEM_HEADS, HEAD_DIM))


def nsa_global(q, q_pos, rows, pe, w1, w2):
    b, t, h, d = q.shape
    L, g = rows.shape[1], rows.shape[3]
    r = h // g
    scale = d ** -0.5
    qg = q.reshape(b, t, g, r, d)
    kc = compress_blocks(rows[:, :, 0], pe[0], w1[0], w2[0])
    vc = compress_blocks(rows[:, :, 1], pe[1], w1[1], w2[1])
    nc = kc.shape[1]
    cmp_end = jnp.arange(nc) * CMP_BLOCK + (CMP_BLOCK - 1)
    cmask = (cmp_end[None, :] <= q_pos[:, None])[None, :, None, None, :]
    sc = jnp.einsum('btgrd,bngd->btgrn', qg, kc) * scale
    pc = masked_softmax(sc, cmask)
    o_cmp = jnp.einsum('btgrn,bngd->btgrd', pc.astype(vc.dtype), vc).reshape(b, t, h, d)
    ratio = SLC_BLOCK // CMP_BLOCK
    ns = -(-L // SLC_BLOCK)
    p_grp = jnp.pad(jnp.sum(pc, axis=3), ((0, 0), (0, 0), (0, 0), (0, ns * ratio - nc)))
    p_slc = p_grp.reshape(b, t, g, ns, ratio).sum(-1)
    blk = jnp.arange(ns)[None, :]
    cur = (q_pos // SLC_BLOCK)[:, None]
    valid = blk * SLC_BLOCK <= q_pos[:, None]
    forced = (blk == 0) | (blk == cur) | (blk == cur - 1)
    score = jnp.where(valid[None, :, None, :], p_slc + jnp.where(forced, FORCE_BONUS, 0.0)[None, :, None, :], -jnp.inf)
    n_sel = min(TOP_N, ns)
    _, sel = lax.top_k(score, n_sel)
    pad = ns * SLC_BLOCK - L
    def to_blocks(x):
        x = jnp.pad(x, ((0, 0), (0, pad), (0, 0), (0, 0)))
        return x.reshape(b, ns, SLC_BLOCK, g, d).transpose(0, 3, 1, 2, 4)
    ks_blk = to_blocks(rows[:, :, 2])
    vs_blk = to_blocks(rows[:, :, 3])
    qc = Q_BLOCK if t % Q_BLOCK == 0 else t
    nq = t // qc
    bi = jnp.arange(b)[:, None, None, None]
    gi = jnp.arange(g)[None, None, :, None]
    def attend(args):
        qb, selb, posb = args
        kb = ks_blk[bi, gi, selb]
        vb = vs_blk[bi, gi, selb]
        kpos = selb[..., None] * SLC_BLOCK + jnp.arange(SLC_BLOCK)
        mask = (kpos <= posb[None, :, None, None, None]).reshape(b, qc, g, 1, n_sel * SLC_BLOCK)
        s = jnp.einsum('bqgrd,bqgnkd->bqgrnk', qb, kb).reshape(b, qc, g, r, n_sel * SLC_BLOCK) * scale
        p = masked_softmax(s, mask).reshape(b, qc, g, r, n_sel, SLC_BLOCK)
        return jnp.einsum('bqgrnk,bqgnkd->bqgrd', p.astype(vb.dtype), vb)
    xs = (qg.reshape(b, nq, qc, g, r, d).swapaxes(0, 1),
          sel.reshape(b, nq, qc, g, n_sel).swapaxes(0, 1),
          q_pos.reshape(nq, qc))
    o_slc = lax.map(attend, xs).swapaxes(0, 1).reshape(b, t, h, d)
    return o_cmp, o_slc


def window_prompt(q, kw, vw):
    b, s, h, d = q.shape
    g = kw.shape[2]
    r = h // g
    nb = s // Q_BLOCK
    npb = WINDOW // Q_BLOCK
    front = ((0, 0), (npb * Q_BLOCK, 0), (0, 0), (0, 0))
    def band(x):
        xp = jnp.pad(x, front)
        return jnp.concatenate([xp[:, j * Q_BLOCK: j * Q_BLOCK + s].reshape(b, nb, Q_BLOCK, g, d) for j in range(npb + 1)], axis=2)
    kb, vb = band(kw), band(vw)
    qb = q.reshape(b, nb, Q_BLOCK, g, r, d)
    sc = jnp.einsum('bnqgrd,bnkgd->bnqgrk', qb, kb) * (d ** -0.5)
    qq = jnp.arange(s).reshape(nb, Q_BLOCK)[:, :, None]
    kk = ((jnp.arange(nb)[:, None] - npb) * Q_BLOCK + jnp.arange((npb + 1) * Q_BLOCK)[None, :])[:, None, :]
    mask = (kk <= qq) & (kk > qq - WINDOW) & (kk >= 0)
    p = masked_softmax(sc, mask[None, :, :, None, None, :])
    o = jnp.einsum('bnqgrk,bnkgd->bnqgrd', p.astype(vb.dtype), vb)
    return o.reshape(b, s, h, d)


def window_sample(q, q_pos, kw, vw, k_pos):
    b, t, h, d = q.shape
    g = kw.shape[2]
    qg = q.reshape(b, t, g, h // g, d)
    s = jnp.einsum('btgrd,bkgd->btgrk', qg, kw) * (d ** -0.5)
    mask = (k_pos[None, :] <= q_pos[:, None]) & (k_pos[None, :] > q_pos[:, None] - WINDOW)
    p = masked_softmax(s, mask[None, :, None, None, :])
    return jnp.einsum('btgrk,bkgd->btgrd', p.astype(vw.dtype), vw).reshape(b, t, h, d)


def nsa_combine(o_cmp, o_slc, o_win, gates):
    gt = jax.nn.sigmoid(gates.astype(jnp.float32))
    o = (gt[..., 0:1] * o_cmp.astype(jnp.float32) + gt[..., 1:2] * o_slc.astype(jnp.float32)
         + gt[..., 2:3] * o_win.astype(jnp.float32))
    b, t = o.shape[:2]
    return o.reshape(b, t, TOK_WIDTH).astype(o_cmp.dtype)


def rotate(x, pos):
    half = x.shape[-1] // 2
    inv = ROPE_BASE ** (-jnp.arange(half, dtype=jnp.float32) / half)
    ang = pos.astype(jnp.float32)[:, None] * inv[None, :]
    cos = jnp.cos(ang)[None, :, None, :]
    sin = jnp.sin(ang)[None, :, None, :]
    xf = x.astype(jnp.float32)
    x1, x2 = xf[..., :half], xf[..., half:]
    return jnp.concatenate([x1 * cos - x2 * sin, x1 * sin + x2 * cos], axis=-1).astype(x.dtype)


def retention_chunkwise(q, k, v, s0):
    b, t, h, _ = q.shape
    dv = v.shape[-1]
    c = RET_CHUNK if t % RET_CHUNK == 0 else t
    n = t // c
    log_g = jnp.log1p(-jnp.exp2(-5.0 - jnp.arange(h, dtype=jnp.float32)))
    i = jnp.arange(c, dtype=jnp.float32)
    diff = i[:, None] - i[None, :]
    dmat = jnp.where(diff >= 0, jnp.exp(jnp.maximum(diff, 0.0)[None] * log_g[:, None, None]), 0.0)
    q_decay = jnp.exp((i + 1.0)[None, :] * log_g[:, None])[..., None]
    k_decay = jnp.exp((c - 1.0 - i)[None, :] * log_g[:, None])[..., None]
    c_decay = jnp.exp(c * log_g)[:, None, None]
    def split(x):
        return x.astype(jnp.float32).reshape(b, n, c, h, x.shape[-1]).transpose(1, 0, 3, 2, 4)
    def step(s, xs):
        qc, kc, vc = xs
        inner = jnp.einsum('bhid,bhjd->bhij', qc, kc) * dmat
        o = jnp.einsum('bhij,bhjv->bhiv', inner, vc) + jnp.einsum('bhid,bhdv->bhiv', qc, s) * q_decay
        s = s * c_decay + jnp.einsum('bhjd,bhjv->bhdv', kc * k_decay, vc)
        return s, o
    s, o = lax.scan(step, s0, (split(q), split(k), split(v)))
    return o.transpose(1, 0, 3, 2, 4).reshape(b, t, h, dv), s


def retention_mix(h, w_in, gn_g, pos, s0):
    b, t, _ = h.shape
    q, k, v, gate, qm = split_cols(h @ w_in, [TOK_WIDTH] * 4 + [MEM_WIDTH])
    hs = (b, t, RET_HEADS, RET_HEAD_DIM)
    q = rotate(q.reshape(hs), pos)
    k = rotate(k.reshape(hs), pos) * (RET_HEAD_DIM ** -0.5)
    o, s_new = retention_chunkwise(q, k, v.reshape(hs), s0.astype(jnp.float32))
    mu = jnp.mean(o, axis=-1, keepdims=True)
    var = jnp.mean(jnp.square(o - mu), axis=-1, keepdims=True)
    y = (o - mu) * lax.rsqrt(var + NORM_EPS) * gn_g.reshape(RET_HEADS, RET_HEAD_DIM).astype(jnp.float32)
    tok = (jax.nn.silu(gate.astype(jnp.float32)) * y.reshape(b, t, TOK_WIDTH)).astype(h.dtype)
    return tok, qm.reshape(b, t, MEM_HEADS, HEAD_DIM), s_new.astype(h.dtype)


def mem_attend(qm, mem_kv):
    b, t = qm.shape[:2]
    s = jnp.einsum('bthd,bmhd->bthm', qm, mem_kv[:, :, 0]) * (HEAD_DIM ** -0.5)
    p = jax.nn.softmax(s.astype(jnp.float32), axis=-1)
    return jnp.einsum('bthm,bmhd->bthd', p.astype(qm.dtype), mem_kv[:, :, 1]).reshape(b, t, MEM_WIDTH)


def conv_ffn(x, buf, norm_g, w_in, conv_w, conv_b, w_out):
    h = rms_norm(x, norm_g)
    a, gv = jnp.split(h @ w_in, 2, axis=-1)
    t = a.shape[1]
    ap = jnp.concatenate([buf.astype(a.dtype), a], axis=1)
    ac = conv_b
    for j in range(CONV_W):
        ac = ac + ap[:, j:j + t] * conv_w[j]
    out = (jax.nn.silu(ac) * gv) @ w_out
    return x + out, ap[:, -(CONV_W - 1):]


def setup_inputs(seed: int = 0) -> dict:
    key = jax.random.key(seed)
    keys = iter(jax.random.split(key, 40))
    f32 = jnp.float32
    n_a = (DEPTH + 1) // 2
    n_b = DEPTH // 2
    n_pages = PAST_LEN // PAGE_SIZE
    n_phys = (5 * DEC_BATCH * n_pages + 3) // 4
    win_buf = min(WINDOW, PAST_LEN)
    def nrm(shape, scale):
        return jax.random.normal(next(keys), shape, f32) * scale
    def gain(shape):
        return 1.0 + nrm(shape, 0.05)
    perm = jax.random.permutation(next(keys), n_phys)[:DEC_BATCH * n_pages]
    page_table = perm.reshape(DEC_BATCH, n_pages).astype(jnp.int32)
    return {
        'x_prompt': nrm((BATCH, SEQ, D_MODEL), 1.0),
        'x_sample': nrm((DEC_BATCH, DEC_SEQ, D_MODEL), 1.0),
        'cache_nsa_kv': nrm((n_a, n_phys, PAGE_SIZE, 4, NSA_KV, HEAD_DIM), 1.0),
        'state_nsa_win': nrm((n_a, DEC_BATCH, win_buf, 2, NSA_KV, HEAD_DIM), 1.0),
        'state_ret': nrm((n_b, DEC_BATCH, RET_HEADS, RET_HEAD_DIM, RET_HEAD_DIM), 0.3),
        'state_ffn_conv': nrm((DEPTH, DEC_BATCH, CONV_W - 1, FFN_DIM), 1.0),
        'cache_mem_kv': nrm((DEPTH, DEC_BATCH, N_MEM, 2, MEM_HEADS, HEAD_DIM), 1.0),
        'page_table': page_table,
        'mem_prompt': nrm((BATCH, N_MEM, D_MODEL), 1.0),
        'norm1_g': gain((DEPTH, D_MODEL)),
        'nsa_w_in': nrm((n_a, D_MODEL, NSA_IN), D_MODEL ** -0.5),
        'nsa_cmp_pe': nrm((n_a, 2, CMP_BLOCK, HEAD_DIM), 0.1),
        'nsa_cmp_w1': nrm((n_a, 2, CMP_BLOCK * HEAD_DIM, HEAD_DIM), (CMP_BLOCK * HEAD_DIM) ** -0.5),
        'nsa_cmp_w2': nrm((n_a, 2, HEAD_DIM, HEAD_DIM), HEAD_DIM ** -0.5),
        'ret_w_in': nrm((n_b, D_MODEL, RET_IN), D_MODEL ** -0.5),
        'ret_gn_g': gain((n_b, TOK_WIDTH)),
        'mem_norm_g': gain((DEPTH, D_MODEL)),
        'w_mem_kv': nrm((DEPTH, D_MODEL, 2 * MEM_WIDTH), D_MODEL ** -0.5),
        'w_o': nrm((DEPTH, MIX_WIDTH, D_MODEL), MIX_WIDTH ** -0.5),
        'norm2_g': gain((DEPTH, D_MODEL)),
        'ffn_w_in': nrm((DEPTH, D_MODEL, 2 * FFN_DIM), D_MODEL ** -0.5),
        'ffn_conv_w': nrm((DEPTH, CONV_W, FFN_DIM), CONV_W ** -0.5),
        'ffn_conv_b': nrm((DEPTH, FFN_DIM), 0.02),
        'ffn_w_out': nrm((DEPTH, FFN_DIM, D_MODEL), FFN_DIM ** -0.5),
        'final_norm_g': gain((D_MODEL,)),
    }


def reference(x_prompt, x_sample, cache_nsa_kv, state_nsa_win, state_ret, state_ffn_conv, cache_mem_kv,
              page_table, mem_prompt, norm1_g, nsa_w_in, nsa_cmp_pe, nsa_cmp_w1, nsa_cmp_w2, ret_w_in,
              ret_gn_g, mem_norm_g, w_mem_kv, w_o, norm2_g, ffn_w_in, ffn_conv_w, ffn_conv_b, ffn_w_out,
              final_norm_g):
    bp, s_len, _ = x_prompt.shape
    db, t_len, _ = x_sample.shape
    past_len = page_table.shape[1] * cache_nsa_kv.shape[2]
    pos_p = jnp.arange(s_len, dtype=jnp.int32)
    pos_s = past_len + jnp.arange(t_len, dtype=jnp.int32)
    wb = state_nsa_win.shape[2]
    win_kpos = past_len - wb + jnp.arange(wb + t_len, dtype=jnp.int32)
    keep_p = min(WINDOW, s_len)
    xp, xs = x_prompt, x_sample
    kv_p_l, kv_s_l, win_p_l, win_s_l, ret_p_l, ret_s_l, conv_p_l, conv_s_l, mem_p_l = ([] for _ in range(9))
    for i in range(DEPTH):
        hp = rms_norm(xp, norm1_g[i])
        hs = rms_norm(xs, norm1_g[i])
        mem_kv_p = (rms_norm(mem_prompt, mem_norm_g[i]) @ w_mem_kv[i]).reshape(bp, N_MEM, 2, MEM_HEADS, HEAD_DIM)
        mem_p_l.append(mem_kv_p)
        if i % 2 == 0:
            a = i // 2
            cmp_w = (nsa_cmp_pe[a], nsa_cmp_w1[a], nsa_cmp_w2[a])
            q_p, rows_p, win_p, gates_p, qm_p = nsa_project(hp, nsa_w_in[a])
            oc, osl = nsa_global(q_p, pos_p, rows_p, *cmp_w)
            ow = window_prompt(q_p, win_p[:, :, 0], win_p[:, :, 1])
            tok_p = nsa_combine(oc, osl, ow, gates_p)
            q_s, rows_s, win_s, gates_s, qm_s = nsa_project(hs, nsa_w_in[a])
            past = cache_nsa_kv[a][page_table].reshape(db, past_len, 4, NSA_KV, HEAD_DIM)
            full = jnp.concatenate([past.astype(rows_s.dtype), rows_s], axis=1)
            oc, osl = nsa_global(q_s, pos_s, full, *cmp_w)
            wfull = jnp.concatenate([state_nsa_win[a].astype(win_s.dtype), win_s], axis=1)
            ow = window_sample(q_s, pos_s, wfull[:, :, 0], wfull[:, :, 1], win_kpos)
            tok_s = nsa_combine(oc, osl, ow, gates_s)
            kv_p_l.append(rows_p)
            kv_s_l.append(rows_s)
            win_p_l.append(win_p[:, s_len - keep_p:])
            win_s_l.append(wfull[:, -wb:])
        else:
            bl = i // 2
            zero_state = jnp.zeros((bp, RET_HEADS, RET_HEAD_DIM, RET_HEAD_DIM), jnp.float32)
            tok_p, qm_p, sp = retention_mix(hp, ret_w_in[bl], ret_gn_g[bl], pos_p, zero_state)
            tok_s, qm_s, ss = retention_mix(hs, ret_w_in[bl], ret_gn_g[bl], pos_s, state_ret[bl])
            ret_p_l.append(sp)
            ret_s_l.append(ss)
        xp = xp + jnp.concatenate([tok_p, mem_attend(qm_p, mem_kv_p)], axis=-1) @ w_o[i]
        xs = xs + jnp.concatenate([tok_s, mem_attend(qm_s, cache_mem_kv[i].astype(qm_s.dtype))], axis=-1) @ w_o[i]
        xp, cp = conv_ffn(xp, jnp.zeros((bp, CONV_W - 1, FFN_DIM), xp.dtype), norm2_g[i], ffn_w_in[i],
                          ffn_conv_w[i], ffn_conv_b[i], ffn_w_out[i])
        xs, cs = conv_ffn(xs, state_ffn_conv[i], norm2_g[i], ffn_w_in[i], ffn_conv_w[i], ffn_conv_b[i], ffn_w_out[i])
        conv_p_l.append(cp)
        conv_s_l.append(cs)
    y_prompt = rms_norm(xp, final_norm_g)
    y_sample = rms_norm(xs, final_norm_g)
    return (y_prompt, y_sample, jnp.stack(kv_p_l), jnp.stack(kv_s_l), jnp.stack(win_p_l), jnp.stack(win_s_l),
            jnp.stack(ret_p_l), jnp.stack(ret_s_l), jnp.stack(conv_p_l), jnp.stack(conv_s_l), jnp.stack(mem_p_l))
```

```python
import functools

import jax
import jax.numpy as jnp
import numpy as np
from jax import lax
from jax.experimental import pallas as pl
from jax.experimental.pallas import tpu as pltpu

F32 = jnp.float32
BF16 = jnp.bfloat16

HEAD_DIM = 128
MEM_HEADS = 4
NSA_KV = 4
CMP_BLOCK = 32
SLC_BLOCK = 64
TOP_N = 16
WINDOW = 512
FORCE_BONUS = 1.0e4
RET_HEAD_DIM = 256
RET_CHUNK = 128
ROPE_BASE = 10000.0
CONV_W = 3
NORM_EPS = 1e-6
PAGE = 128

V7X_LANES = 128
V7X_VMEM_BYTES = 64 * 1024 * 1024
NEG_BIG = -1e30


def _cparams(n_axes, vmem_bytes):
    limit = min(int(vmem_bytes * 1.25) + (8 << 20), V7X_VMEM_BYTES - (4 << 20))
    return pltpu.CompilerParams(dimension_semantics=("arbitrary",) * n_axes, vmem_limit_bytes=limit)


def _dot(a, b):
    return jnp.dot(a, b, preferred_element_type=F32)


def _dot_nt(a, b):
    return lax.dot_general(a, b, (((1,), (1,)), ((), ())), preferred_element_type=F32)


def _rms(x, g):
    ms = jnp.mean(x * x, axis=-1, keepdims=True)
    return x * lax.rsqrt(ms + NORM_EPS) * g


def _norm_matmul_kernel(x_ref, g_ref, w_ref, o_ref, hn_ref):
    @pl.when(pl.program_id(1) == 0)
    def _():
        hn_ref[...] = _rms(x_ref[...], g_ref[...]).astype(BF16)

    o_ref[...] = _dot(hn_ref[...], w_ref[...])


def _norm_matmul(x, g, w, tm, tn):
    m, k = x.shape
    n = w.shape[1]
    vmem = 2 * tm * k * 4 + tm * k * 2 + 2 * k * tn * 2 + 2 * tm * tn * 4
    return pl.pallas_call(
        _norm_matmul_kernel,
        grid=(m // tm, n // tn),
        in_specs=[
            pl.BlockSpec((tm, k), lambda i, j: (i, 0)),
            pl.BlockSpec((1, k), lambda i, j: (0, 0)),
            pl.BlockSpec((k, tn), lambda i, j: (0, j)),
        ],
        out_specs=pl.BlockSpec((tm, tn), lambda i, j: (i, j)),
        out_shape=jax.ShapeDtypeStruct((m, n), F32),
        scratch_shapes=[pltpu.VMEM((tm, k), BF16)],
        compiler_params=_cparams(2, vmem),
        name="norm_matmul",
    )(x, g.reshape(1, k), w)


def _wo_kernel(x_ref, tok_ref, mem_ref, wt_ref, wm_ref, o_ref):
    o_ref[...] = (x_ref[...] + _dot(tok_ref[...].astype(BF16), wt_ref[...])
                  + _dot(mem_ref[...].astype(BF16), wm_ref[...]))


def _wo(x, tok, mem, w_tok, w_mem, tm, tn):
    m, d = x.shape
    kt, km = tok.shape[1], mem.shape[1]
    vmem = 2 * (tm * tn * 8 + tm * (kt + km) * 4 + (kt + km) * tn * 2)
    return pl.pallas_call(
        _wo_kernel,
        grid=(m // tm, d // tn),
        in_specs=[
            pl.BlockSpec((tm, tn), lambda i, j: (i, j)),
            pl.BlockSpec((tm, kt), lambda i, j: (i, 0)),
            pl.BlockSpec((tm, km), lambda i, j: (i, 0)),
            pl.BlockSpec((kt, tn), lambda i, j: (0, j)),
            pl.BlockSpec((km, tn), lambda i, j: (0, j)),
        ],
        out_specs=pl.BlockSpec((tm, tn), lambda i, j: (i, j)),
        out_shape=jax.ShapeDtypeStruct((m, d), F32),
        compiler_params=_cparams(2, vmem),
        name="wo_residual",
    )(x, tok, mem, w_tok, w_mem)


def _ffn_kernel(x_ref, g_ref, wa_ref, wg_ref, cw_ref, cb_ref, wo_ref, buf_ref, fg_ref,
                y_ref, tail_ref, hn_ref, carry_ref, *, tiles_per_seq, n_f, final_norm):
    i = pl.program_id(0)
    f = pl.program_id(1)
    tm = x_ref.shape[0]

    @pl.when(f == 0)
    def _():
        hn_ref[...] = _rms(x_ref[...], g_ref[...]).astype(BF16)

    hn = hn_ref[...]
    a = _dot(hn, wa_ref[...])
    gv = _dot(hn, wg_ref[...])
    seq_start = (i % tiles_per_seq) == 0
    prev = jnp.where(seq_start, buf_ref[0], carry_ref[f])
    row = lax.broadcasted_iota(jnp.int32, a.shape, 0)
    a1 = jnp.where(row == 0, prev[1:2], pltpu.roll(a, 1, axis=0))
    a2 = jnp.where(row == 0, prev[0:1], jnp.where(row == 1, prev[1:2], pltpu.roll(a, 2, axis=0)))
    cw = cw_ref[...]
    ac = cb_ref[...] + a2 * cw[0:1] + a1 * cw[1:2] + a * cw[2:3]
    act = (ac * jax.nn.sigmoid(ac) * gv).astype(BF16)
    last2 = a[tm - 2:tm]
    carry_ref[f] = last2
    tail_ref[0] = last2
    contrib = _dot(act, wo_ref[...])

    @pl.when(f == 0)
    def _():
        y_ref[...] = x_ref[...] + contrib

    @pl.when(f != 0)
    def _():
        y_ref[...] += contrib

    if final_norm:
        @pl.when(f == n_f - 1)
        def _():
            y_ref[...] = _rms(y_ref[...], fg_ref[...])


def _ffn(x, norm_g, w_in, conv_w, conv_b, w_out, buf, final_g, *, seq_len, tm, tf, final_norm):
    m, d = x.shape
    fdim = w_out.shape[0]
    nf = fdim // tf
    tiles_per_seq = seq_len // tm
    vmem = (4 * tm * d * 4 + tm * d * 2 + 2 * (2 * d * tf * 2 + tf * d * 2) + 6 * tm * tf * 4
            + nf * 8 * tf * 4)
    kern = functools.partial(_ffn_kernel, tiles_per_seq=tiles_per_seq, n_f=nf, final_norm=final_norm)
    return pl.pallas_call(
        kern,
        grid=(m // tm, nf),
        in_specs=[
            pl.BlockSpec((tm, d), lambda i, f: (i, 0)),
            pl.BlockSpec((1, d), lambda i, f: (0, 0)),
            pl.BlockSpec((d, tf), lambda i, f: (0, f)),
            pl.BlockSpec((d, tf), lambda i, f: (0, f + nf)),
            pl.BlockSpec((CONV_W, tf), lambda i, f: (0, f)),
            pl.BlockSpec((1, tf), lambda i, f: (0, f)),
            pl.BlockSpec((tf, d), lambda i, f: (f, 0)),
            pl.BlockSpec((1, CONV_W - 1, tf), lambda i, f: (i // tiles_per_seq, 0, f)),
            pl.BlockSpec((1, d), lambda i, f: (0, 0)),
        ],
        out_specs=[
            pl.BlockSpec((tm, d), lambda i, f: (i, 0)),
            pl.BlockSpec((1, CONV_W - 1, tf), lambda i, f: (i, 0, f)),
        ],
        out_shape=[jax.ShapeDtypeStruct((m, d), F32),
                   jax.ShapeDtypeStruct((m // tm, CONV_W - 1, fdim), F32)],
        scratch_shapes=[pltpu.VMEM((tm, d), BF16), pltpu.VMEM((nf, CONV_W - 1, tf), F32)],
        compiler_params=_cparams(2, vmem),
        name="conv_ffn",
    )(x, norm_g.reshape(1, d), w_in, w_in, conv_w, conv_b.reshape(1, fdim), w_out, buf,
      final_g.reshape(1, d))


def _mem_kernel(q_ref, kv_ref, o_ref):
    q = q_ref[0] * (HEAD_DIM ** -0.5)
    kv = kv_ref[0]
    width = MEM_HEADS * HEAD_DIM
    outs = []
    for h in range(MEM_HEADS):
        lo = h * HEAD_DIM
        qh = q[:, lo:lo + HEAD_DIM].astype(BF16)
        kh = kv[:, lo:lo + HEAD_DIM].astype(BF16)
        vh = kv[:, width + lo:width + lo + HEAD_DIM].astype(BF16)
        s = _dot_nt(qh, kh)
        e = jnp.exp(s - jnp.max(s, axis=-1, keepdims=True))
        p = e / jnp.sum(e, axis=-1, keepdims=True)
        outs.append(_dot(p.astype(BF16), vh))
    o_ref[0] = jnp.concatenate(outs, axis=-1)


def _mem_attend(proj, qm_col_block, mem_kv, tq):
    b, t, _ = proj.shape
    width = MEM_HEADS * HEAD_DIM
    n_mem = mem_kv.shape[1]
    vmem = 2 * (2 * tq * width * 4 + n_mem * 2 * width * 4) + 8 * tq * n_mem * 4
    return pl.pallas_call(
        _mem_kernel,
        grid=(b, t // tq),
        in_specs=[
            pl.BlockSpec((1, tq, width), lambda bi, i: (bi, i, qm_col_block)),
            pl.BlockSpec((1, n_mem, 2 * width), lambda bi, i: (bi, 0, 0)),
        ],
        out_specs=pl.BlockSpec((1, tq, width), lambda bi, i: (bi, i, 0)),
        out_shape=jax.ShapeDtypeStruct((b, t, width), F32),
        compiler_params=_cparams(2, vmem),
        name="mem_attend",
    )(proj, mem_kv)


def _ret_kernel(q_ref, k_ref, v_ref, gate_ref, cos_ref, sin_ref, dmat_ref, qd_ref, kd_ref, cd_ref,
                gn_ref, s0_ref, tok_ref, s_out_ref, s_ref, *, n_chunks):
    c = pl.program_id(2)

    @pl.when(c == 0)
    def _():
        s_ref[...] = s0_ref[0, 0]

    half = RET_HEAD_DIM // 2
    cos = cos_ref[...]
    sin = sin_ref[...]

    def rot(x):
        x1, x2 = x[:, :half], x[:, half:]
        return jnp.concatenate([x1 * cos - x2 * sin, x1 * sin + x2 * cos], axis=-1)

    q = rot(q_ref[0])
    k = rot(k_ref[0]) * (RET_HEAD_DIM ** -0.5)
    v = v_ref[0].astype(BF16)
    qb = q.astype(BF16)
    s_old = s_ref[...]
    inner = _dot_nt(qb, k.astype(BF16)) * dmat_ref[0]
    o = _dot(inner.astype(BF16), v) + _dot(qb, s_old.astype(BF16)) * qd_ref[0]
    kd = (k * kd_ref[0]).astype(BF16)
    s_new = s_old * cd_ref[0] + _dot(kd.T, v)
    s_ref[...] = s_new

    mu = jnp.mean(o, axis=-1, keepdims=True)
    dev = o - mu
    var = jnp.mean(dev * dev, axis=-1, keepdims=True)
    y = dev * lax.rsqrt(var + NORM_EPS) * gn_ref[0]
    gate = gate_ref[0]
    tok_ref[0] = gate * jax.nn.sigmoid(gate) * y

    @pl.when(c == n_chunks - 1)
    def _():
        s_out_ref[0, 0] = s_new


def _retention(proj, gn_g, s0, cos, sin, dmat, q_decay, k_decay, c_decay):
    b, t, _ = proj.shape
    h = s0.shape[1]
    dk = RET_HEAD_DIM
    c = dmat.shape[1]
    n = t // c
    vmem = 2 * (5 * c * dk * 4 + c * dk * 4 + c * c * 4 + 2 * c * 128 * 4 + 2 * dk * dk * 4) + 3 * dk * dk * 4
    blk = lambda off: pl.BlockSpec((1, c, dk), lambda bi, hi, ci: (bi, ci, off * h + hi))
    return pl.pallas_call(
        functools.partial(_ret_kernel, n_chunks=n),
        grid=(b, h, n),
        in_specs=[
            blk(0), blk(1), blk(2), blk(3),
            pl.BlockSpec((c, dk // 2), lambda bi, hi, ci: (ci, 0)),
            pl.BlockSpec((c, dk // 2), lambda bi, hi, ci: (ci, 0)),
            pl.BlockSpec((1, c, c), lambda bi, hi, ci: (hi, 0, 0)),
            pl.BlockSpec((1, c, 1), lambda bi, hi, ci: (hi, 0, 0)),
            pl.BlockSpec((1, c, 1), lambda bi, hi, ci: (hi, 0, 0)),
            pl.BlockSpec((1, 1, dk), lambda bi, hi, ci: (hi, 0, 0)),
            pl.BlockSpec((1, 1, dk), lambda bi, hi, ci: (hi, 0, 0)),
            pl.BlockSpec((1, 1, dk, dk), lambda bi, hi, ci: (bi, hi, 0, 0)),
        ],
        out_specs=[
            pl.BlockSpec((1, c, dk), lambda bi, hi, ci: (bi, ci, hi)),
            pl.BlockSpec((1, 1, dk, dk), lambda bi, hi, ci: (bi, hi, 0, 0)),
        ],
        out_shape=[jax.ShapeDtypeStruct((b, t, h * dk), F32),
                   jax.ShapeDtypeStruct((b, h, dk, dk), F32)],
        scratch_shapes=[pltpu.VMEM((dk, dk), F32)],
        compiler_params=_cparams(3, vmem),
        name="retention",
    )(proj, proj, proj, proj, cos, sin, dmat, q_decay, k_decay, c_decay, gn_g.reshape(h, 1, dk), s0)


def _retention_tables(n_heads, c_pad, c_real, pos):
    log_g = jnp.log1p(-jnp.exp2(-5.0 - jnp.arange(n_heads, dtype=F32)))
    i = jnp.arange(c_pad, dtype=F32)
    live = i < c_real
    diff = i[:, None] - i[None, :]
    dmat = jnp.where((diff >= 0) & live[:, None] & live[None, :],
                     jnp.exp(jnp.maximum(diff, 0.0)[None] * log_g[:, None, None]), 0.0)
    q_decay = jnp.exp((i + 1.0)[None, :] * log_g[:, None])[..., None]
    k_decay = jnp.where(live[None, :], jnp.exp((c_real - 1.0 - i)[None, :] * log_g[:, None]), 0.0)[..., None]
    c_decay = jnp.broadcast_to(jnp.exp(c_real * log_g)[:, None, None], (n_heads, 1, RET_HEAD_DIM))
    half = RET_HEAD_DIM // 2
    inv = ROPE_BASE ** (-jnp.arange(half, dtype=F32) / half)
    ang = pos.astype(F32)[:, None] * inv[None, :]
    return jnp.cos(ang), jnp.sin(ang), dmat, q_decay, k_decay, c_decay


def _compress_kernel(plist_ref, src_ref, pe_ref, w1_ref, w2_ref, o_ref, buf_ref, col_ref, sem_ref, *,
                     pages, col0, n_steps):
    s = pl.program_id(0)
    slot = s % 2
    kv_w = NSA_KV * HEAD_DIM
    blocks = pages * (PAGE // CMP_BLOCK)

    def copies(step, to_slot):
        return [pltpu.make_async_copy(src_ref.at[plist_ref[step * pages + p], :, pl.ds(col0, 2 * kv_w)],
                                      buf_ref.at[to_slot, pl.ds(p * PAGE, PAGE), :], sem_ref.at[to_slot])
                for p in range(pages)]

    @pl.when(s == 0)
    def _():
        for cp in copies(0, 0):
            cp.start()

    @pl.when(s + 1 < n_steps)
    def _():
        for cp in copies(s + 1, 1 - slot):
            cp.start()

    for cp in copies(s, slot):
        cp.wait()

    for cb in range(2 * NSA_KV):
        col_ref[cb] = buf_ref[slot, :, cb * HEAD_DIM:(cb + 1) * HEAD_DIM]

    for kind in range(2):
        acc = jnp.zeros((NSA_KV * blocks, HEAD_DIM), F32)
        for cc in range(CMP_BLOCK // 2):
            parts = []
            for g in range(NSA_KV):
                cb = kind * NSA_KV + g
                xa = col_ref[cb, pl.ds(2 * cc, blocks, stride=CMP_BLOCK), :]
                xb = col_ref[cb, pl.ds(2 * cc + 1, blocks, stride=CMP_BLOCK), :]
                xa = xa + pe_ref[kind, 2 * cc:2 * cc + 1, :]
                xb = xb + pe_ref[kind, 2 * cc + 1:2 * cc + 2, :]
                parts.append(jnp.concatenate([xa, xb], axis=1).astype(BF16))
            acc = acc + _dot(jnp.concatenate(parts, axis=0), w1_ref[kind, cc])
        out = _dot(jax.nn.gelu(acc).astype(BF16), w2_ref[kind])
        for g in range(NSA_KV):
            o_ref[kind, g] = out[g * blocks:(g + 1) * blocks]


def _compress(src, plist, col0, pe, w1, w2, pages):
    n_pages = plist.shape[0]
    blocks = pages * (PAGE // CMP_BLOCK)
    kv_w = NSA_KV * HEAD_DIM
    vmem = 3 * pages * PAGE * 2 * kv_w * 4 + 4 * CMP_BLOCK * HEAD_DIM * HEAD_DIM * 2 + 16 * blocks * NSA_KV * HEAD_DIM * 4
    grid_spec = pltpu.PrefetchScalarGridSpec(
        num_scalar_prefetch=1,
        grid=(n_pages // pages,),
        in_specs=[
            pl.BlockSpec(memory_space=pl.ANY),
            pl.BlockSpec((2, CMP_BLOCK, HEAD_DIM), lambda s, pr: (0, 0, 0)),
            pl.BlockSpec((2, CMP_BLOCK // 2, 2 * HEAD_DIM, HEAD_DIM), lambda s, pr: (0, 0, 0, 0)),
            pl.BlockSpec((2, HEAD_DIM, HEAD_DIM), lambda s, pr: (0, 0, 0)),
        ],
        out_specs=pl.BlockSpec((2, NSA_KV, blocks, HEAD_DIM), lambda s, pr: (0, 0, s, 0)),
        scratch_shapes=[pltpu.VMEM((2, pages * PAGE, 2 * kv_w), F32),
                        pltpu.VMEM((2 * NSA_KV, pages * PAGE, HEAD_DIM), F32), pltpu.SemaphoreType.DMA((2,))],
    )
    return pl.pallas_call(
        functools.partial(_compress_kernel, pages=pages, col0=col0, n_steps=n_pages // pages),
        grid_spec=grid_spec,
        out_shape=jax.ShapeDtypeStruct((2, NSA_KV, n_pages * (PAGE // CMP_BLOCK), HEAD_DIM), F32),
        compiler_params=_cparams(1, vmem),
        name="nsa_compress",
    )(plist, src, pe, w1.reshape(2, CMP_BLOCK // 2, 2 * HEAD_DIM, HEAD_DIM), w2)


def _even_odd(c, n_seq):
    _, g, total, d = c.shape
    nc = total // n_seq
    c = c.reshape(2, g, n_seq, nc // 2, 2, d).transpose(0, 1, 2, 4, 3, 5)
    return c.reshape(2, g, n_seq, nc, d).astype(BF16)


def _cmp_attend(qb, kc, vc, qpos_col):
    nc = kc.shape[0]
    lane = lax.broadcasted_iota(jnp.int32, (1, nc), 1)
    blk = jnp.where(lane < nc // 2, 2 * lane, 2 * (lane - nc // 2) + 1)
    vis = (blk * CMP_BLOCK + (CMP_BLOCK - 1)) <= qpos_col
    s = jnp.where(vis, _dot_nt(qb, kc), -jnp.inf)
    m = jnp.max(s, axis=-1, keepdims=True)
    m = jnp.where(m > -jnp.inf, m, 0.0)
    e = jnp.where(vis, jnp.exp(s - m), 0.0)
    p = e / jnp.maximum(jnp.sum(e, axis=-1, keepdims=True), 1e-30)
    return _dot(p.astype(BF16), vc), p


def _select_blocks(p_t, qpos_row, n_sel, m_hi, score_ref):
    nb, nq = p_t.shape
    blk = lax.broadcasted_iota(jnp.int32, (nb, nq), 0)
    valid = blk * SLC_BLOCK <= qpos_row
    cur = qpos_row // SLC_BLOCK
    forced = (blk == 0) | (blk == cur) | (blk == cur - 1)
    score = jnp.where(valid, p_t + jnp.where(forced, FORCE_BONUS, 0.0), -jnp.inf)
    score_ref[...] = score
    rows = min(64, nb)
    outs = []
    for r0 in range(0, nb, rows):
        sc = score[r0:r0 + rows]
        bk = blk[r0:r0 + rows]

        def body(mi, rank, sc=sc, bk=bk):
            other = score_ref[pl.ds(mi, 1), :]
            ge = jnp.where(other >= sc, 1.0, 0.0)
            gt = jnp.where(other > sc, 1.0, 0.0)
            return rank + jnp.where(bk > mi, ge, gt)

        rank = lax.fori_loop(0, m_hi, body, jnp.zeros_like(sc))
        outs.append(jnp.where((rank < n_sel) & valid[r0:r0 + rows], 1.0, 0.0))
    return jnp.concatenate(outs, axis=0) if len(outs) > 1 else outs[0]


def _softmax_step(s, vis, v, m_ref, l_ref, acc_ref, r0):
    rows = s.shape[0]
    sl = pl.ds(r0, rows)
    s = jnp.where(vis, s, NEG_BIG)
    m_old = m_ref[sl, :]
    m_new = jnp.maximum(m_old, jnp.max(s, axis=-1, keepdims=True))
    alpha = jnp.exp(m_old - m_new)
    p = jnp.exp(s - m_new)
    l_ref[sl, :] = alpha * l_ref[sl, :] + jnp.sum(p, axis=-1, keepdims=True)
    acc_ref[sl, :] = alpha * acc_ref[sl, :] + _dot(p.astype(BF16), v)
    m_ref[sl, :] = m_new


def _softmax_finish(m_ref, l_ref, acc_ref):
    seen = m_ref[...] > 0.5 * NEG_BIG
    return jnp.where(seen, acc_ref[...] / jnp.maximum(l_ref[...], 1e-30), 0.0)


def _softmax_once(s, vis, v):
    s = jnp.where(vis, s, NEG_BIG)
    m = jnp.max(s, axis=-1, keepdims=True)
    e = jnp.where(vis, jnp.exp(s - m), 0.0)
    p = e / jnp.maximum(jnp.sum(e, axis=-1, keepdims=True), 1e-30)
    return _dot(p.astype(BF16), v)


def _block_expander(n_blocks, k0, n_keys):
    blk = lax.broadcasted_iota(jnp.int32, (n_blocks, n_keys), 0)
    key = k0 + lax.broadcasted_iota(jnp.int32, (n_blocks, n_keys), 1)
    return jnp.where(blk == key // SLC_BLOCK, 1.0, 0.0).astype(BF16)


def _nsa_prompt_kernel(q_ref, kc_ref, vc_ref, ks_ref, vs_ref, kw_ref, vw_ref, gate_ref, tok_ref,
                       score_ref, m_ref, l_ref, acc_ref, *, tk):
    i = pl.program_id(2)
    tq = q_ref.shape[1]
    r = q_ref.shape[2] // HEAD_DIM
    nb = kc_ref.shape[2] // (SLC_BLOCK // CMP_BLOCK)
    nbp = -(-nb // V7X_LANES) * V7X_LANES
    t0 = i * tq
    q = q_ref[0] * (HEAD_DIM ** -0.5)
    qb = jnp.concatenate([q[:, h * HEAD_DIM:(h + 1) * HEAD_DIM] for h in range(r)], axis=0).astype(BF16)
    tcol = t0 + lax.broadcasted_iota(jnp.int32, (tq, 1), 0)
    trow = t0 + lax.broadcasted_iota(jnp.int32, (1, tq), 1)

    o_cmp, p = _cmp_attend(qb, kc_ref[0, 0], vc_ref[0, 0], jnp.concatenate([tcol] * r, axis=0))
    p_grp = p[0:tq]
    for h in range(1, r):
        p_grp = p_grp + p[h * tq:(h + 1) * tq]
    p_pair = p_grp + pltpu.roll(p_grp, nb, axis=1)
    m_hi = jnp.minimum((t0 + tq - 1) // SLC_BLOCK + 1, nb)
    sel_t = _select_blocks(p_pair.T[:nb], trow, TOP_N, m_hi, score_ref)
    if nbp > nb:
        sel_t = jnp.concatenate([sel_t, jnp.zeros((nbp - nb, tq), F32)], axis=0)
    sel = sel_t.T.astype(BF16)

    m_ref[...] = jnp.full(m_ref.shape, NEG_BIG, F32)
    l_ref[...] = jnp.zeros(l_ref.shape, F32)
    acc_ref[...] = jnp.zeros(acc_ref.shape, F32)

    def key_tile(j, carry):
        k0 = pl.multiple_of(j * tk, tk)
        kt = ks_ref[0, pl.ds(k0, tk), :].astype(BF16)
        vt = vs_ref[0, pl.ds(k0, tk), :].astype(BF16)
        s = _dot_nt(qb, kt)
        picked = _dot(sel, _block_expander(nbp, k0, tk))
        kpos = k0 + lax.broadcasted_iota(jnp.int32, (tq, tk), 1)
        vis = (picked > 0.5) & (kpos <= tcol)
        for h in range(r):
            _softmax_step(s[h * tq:(h + 1) * tq], vis, vt, m_ref, l_ref, acc_ref, h * tq)
        return carry

    lax.fori_loop(0, (t0 + tq + tk - 1) // tk, key_tile, 0)
    o_slc = _softmax_finish(m_ref, l_ref, acc_ref)

    span = WINDOW + tq
    w0 = pl.multiple_of(jnp.maximum(t0 + tq - span, 0), tq)
    kw = kw_ref[0, pl.ds(w0, span), :].astype(BF16)
    vw = vw_ref[0, pl.ds(w0, span), :].astype(BF16)
    s = _dot_nt(qb, kw)
    kpos = w0 + lax.broadcasted_iota(jnp.int32, (tq, span), 1)
    vis = (kpos <= tcol) & (kpos > tcol - WINDOW)
    o_win = [_softmax_once(s[h * tq:(h + 1) * tq], vis, vw) for h in range(r)]

    gt = jax.nn.sigmoid(gate_ref[0])
    outs = []
    for h in range(r):
        rows = slice(h * tq, (h + 1) * tq)
        outs.append(gt[:, 3 * h:3 * h + 1] * o_cmp[rows] + gt[:, 3 * h + 1:3 * h + 2] * o_slc[rows]
                    + gt[:, 3 * h + 2:3 * h + 3] * o_win[h])
    tok_ref[0] = jnp.concatenate(outs, axis=-1)


def _nsa_prompt(proj, kc, vc, cols, tq, tk):
    b, t, _ = proj.shape
    g = kc.shape[0]
    nc = kc.shape[2]
    r = 3
    nb = nc // (SLC_BLOCK // CMP_BLOCK)
    assert t >= WINDOW + tq and t % tk == 0 and tk % tq == 0 and nb % 64 == 0 and nc <= V7X_LANES
    kv_spec = lambda c0: pl.BlockSpec((1, t, HEAD_DIM), lambda bi, gi, i: (bi, 0, c0 + gi))
    cmp_spec = pl.BlockSpec((1, 1, nc, HEAD_DIM), lambda bi, gi, i: (gi, bi, 0, 0))
    vmem = 2 * 4 * t * HEAD_DIM * 4 + 40 * r * tq * max(tk, WINDOW + tq) * 4
    return pl.pallas_call(
        functools.partial(_nsa_prompt_kernel, tk=tk),
        grid=(b, g, t // tq),
        in_specs=[
            pl.BlockSpec((1, tq, r * HEAD_DIM), lambda bi, gi, i: (bi, i, cols["q"] // r + gi)),
            cmp_spec, cmp_spec,
            kv_spec(cols["ks"]), kv_spec(cols["vs"]), kv_spec(cols["kw"]), kv_spec(cols["vw"]),
            pl.BlockSpec((1, tq, HEAD_DIM), lambda bi, gi, i: (bi, i, cols["gates"] + gi)),
        ],
        out_specs=pl.BlockSpec((1, tq, r * HEAD_DIM), lambda bi, gi, i: (bi, i, gi)),
        out_shape=jax.ShapeDtypeStruct((b, t, g * r * HEAD_DIM), F32),
        scratch_shapes=[pltpu.VMEM((nb, tq), F32), pltpu.VMEM((r * tq, 1), F32), pltpu.VMEM((r * tq, 1), F32),
                        pltpu.VMEM((r * tq, HEAD_DIM), F32)],
        compiler_params=_cparams(3, vmem),
        name="nsa_prompt",
    )(proj, kc, vc, proj, proj, proj, proj, proj)


NSA_HEADS = 12
NSA_KVW = NSA_KV * HEAD_DIM
NSA_Q_W = NSA_HEADS * HEAD_DIM
NSA_ROWS_W = 4 * NSA_KVW
NSA_WIN_W = 2 * NSA_KVW
MEM_W = MEM_HEADS * HEAD_DIM
NSA_GATES = 3 * NSA_HEADS
NSA_MAIN_W = NSA_Q_W + NSA_ROWS_W + NSA_WIN_W
NSA_PROJ_W = NSA_MAIN_W + MEM_W + NSA_KV * V7X_LANES
NSA_COLS = {
    "q": 0,
    "ks": (NSA_Q_W + 2 * NSA_KVW) // V7X_LANES,
    "vs": (NSA_Q_W + 3 * NSA_KVW) // V7X_LANES,
    "kw": (NSA_Q_W + NSA_ROWS_W) // V7X_LANES,
    "vw": (NSA_Q_W + NSA_ROWS_W + NSA_KVW) // V7X_LANES,
    "gates": (NSA_MAIN_W + MEM_W) // V7X_LANES,
}


def _prep_nsa_w(w):
    d = w.shape[0]
    gates = w[:, NSA_MAIN_W:NSA_MAIN_W + NSA_GATES].reshape(d, NSA_KV, NSA_GATES // NSA_KV)
    gates = jnp.pad(gates, ((0, 0), (0, 0), (0, V7X_LANES - NSA_GATES // NSA_KV))).reshape(d, NSA_KV * V7X_LANES)
    return jnp.concatenate([w[:, :NSA_MAIN_W], w[:, NSA_MAIN_W + NSA_GATES:], gates], axis=1).astype(BF16)


def _stack_heads(q, n_heads):
    return jnp.concatenate([q[:, h * HEAD_DIM:(h + 1) * HEAD_DIM] for h in range(n_heads)], axis=0)


def _nsa_sample_cmp_kernel(q_ref, kc_ref, vc_ref, ocmp_ref, pslc_ref, *, past):
    t = q_ref.shape[1]
    g_n = kc_ref.shape[0]
    r = q_ref.shape[2] // HEAD_DIM // g_n
    nb = kc_ref.shape[2] // (SLC_BLOCK // CMP_BLOCK)
    q = q_ref[0] * (HEAD_DIM ** -0.5)
    qpos = past + lax.broadcasted_iota(jnp.int32, (t, 1), 0)
    qpos_r = jnp.concatenate([qpos] * r, axis=0)
    for g in range(g_n):
        qb = _stack_heads(q[:, g * r * HEAD_DIM:(g + 1) * r * HEAD_DIM], r).astype(BF16)
        o, p = _cmp_attend(qb, kc_ref[g, 0], vc_ref[g, 0], qpos_r)
        ocmp_ref[0, g * r * t:(g + 1) * r * t, :] = o
        p_grp = p[0:t]
        for h in range(1, r):
            p_grp = p_grp + p[h * t:(h + 1) * t]
        p_pair = p_grp + pltpu.roll(p_grp, nb, axis=1)
        pslc_ref[0, g * t:(g + 1) * t, :] = p_pair[:, :nb]


def _nsa_sample_cmp(proj, kc, vc, past):
    b, t, _ = proj.shape
    g, _, nc, _ = kc.shape
    nb = nc // (SLC_BLOCK // CMP_BLOCK)
    cmp_spec = pl.BlockSpec((g, 1, nc, HEAD_DIM), lambda bi: (0, bi, 0, 0))
    vmem = 2 * (t * NSA_Q_W * 4 + 2 * g * nc * HEAD_DIM * 2) + 64 * nc * 4 * 8
    return pl.pallas_call(
        functools.partial(_nsa_sample_cmp_kernel, past=past),
        grid=(b,),
        in_specs=[pl.BlockSpec((1, t, NSA_Q_W), lambda bi: (bi, 0, 0)), cmp_spec, cmp_spec],
        out_specs=[pl.BlockSpec((1, NSA_HEADS * t, HEAD_DIM), lambda bi: (bi, 0, 0)),
                   pl.BlockSpec((1, g * t, nb), lambda bi: (bi, 0, 0))],
        out_shape=[jax.ShapeDtypeStruct((b, NSA_HEADS * t, HEAD_DIM), F32),
                   jax.ShapeDtypeStruct((b, g * t, nb), F32)],
        compiler_params=_cparams(1, vmem),
        name="nsa_sample_cmp",
    )(proj, kc, vc)


def _rank_kernel(p_ref, qpos_ref, sel_ref, score_ref, *, n_sel):
    sel_ref[...] = _select_blocks(p_ref[...], qpos_ref[...], n_sel, p_ref.shape[0], score_ref)


def _rank_blocks(p_t, qpos, n_sel):
    nb, nq = p_t.shape
    tq = V7X_LANES
    return pl.pallas_call(
        functools.partial(_rank_kernel, n_sel=n_sel),
        grid=(nq // tq,),
        in_specs=[pl.BlockSpec((nb, tq), lambda i: (0, i)), pl.BlockSpec((1, tq), lambda i: (0, i))],
        out_specs=pl.BlockSpec((nb, tq), lambda i: (0, i)),
        out_shape=jax.ShapeDtypeStruct((nb, nq), F32),
        scratch_shapes=[pltpu.VMEM((nb, tq), F32)],
        compiler_params=_cparams(1, 8 * nb * tq * 4),
        name="nsa_rank",
    )(p_t, qpos)


def _nsa_sample_attend_kernel(pages_ref, flags_ref, kv_ref, q_ref, sel_ref, ocmp_ref, ksn_ref, vsn_ref,
                              kwn_ref, vwn_ref, wst_ref, gate_ref, tok_ref, qs_ref, m_ref, l_ref, acc_ref, *,
                              past, npg):
    b = pl.program_id(0)
    j = pl.program_id(1)
    t = q_ref.shape[1]
    g_n = NSA_KV
    r = NSA_HEADS // g_n
    rg = r * t
    nb = sel_ref.shape[2]

    @pl.when(j == 0)
    def _():
        qs_ref[...] = _stack_heads(q_ref[0] * (HEAD_DIM ** -0.5), NSA_HEADS)
        m_ref[...] = jnp.full(m_ref.shape, NEG_BIG, F32)
        l_ref[...] = jnp.zeros(l_ref.shape, F32)
        acc_ref[...] = jnp.zeros(acc_ref.shape, F32)

    @pl.when(flags_ref[b * npg + j] != 0)
    def _():
        kv = kv_ref[0]
        picked = _dot(sel_ref[0], _block_expander(nb, j * PAGE, PAGE))
        for g in range(g_n):
            kg = kv[:, g * HEAD_DIM:(g + 1) * HEAD_DIM].astype(BF16)
            vg = kv[:, NSA_KVW + g * HEAD_DIM:NSA_KVW + (g + 1) * HEAD_DIM].astype(BF16)
            s = _dot_nt(qs_ref[g * rg:(g + 1) * rg, :].astype(BF16), kg)
            vis = jnp.concatenate([picked[g * t:(g + 1) * t]] * r, axis=0) > 0.5
            _softmax_step(s, vis, vg, m_ref, l_ref, acc_ref, g * rg)

    @pl.when(j == npg - 1)
    def _():
        tcol = jnp.concatenate([lax.broadcasted_iota(jnp.int32, (t, 1), 0)] * r, axis=0)
        pad = jnp.zeros((PAGE - t, HEAD_DIM), F32)

        def new_keys(ref, g):
            return jnp.concatenate([ref[0][:, g * HEAD_DIM:(g + 1) * HEAD_DIM], pad], axis=0)

        lane = lax.broadcasted_iota(jnp.int32, (rg, PAGE), 1)
        for g in range(g_n):
            qg = qs_ref[g * rg:(g + 1) * rg, :].astype(BF16)
            _softmax_step(_dot_nt(qg, new_keys(ksn_ref, g).astype(BF16)), lane <= tcol,
                          new_keys(vsn_ref, g).astype(BF16), m_ref, l_ref, acc_ref, g * rg)
        o_slc = _softmax_finish(m_ref, l_ref, acc_ref)

        wb = wst_ref.shape[1]
        kpos = past - wb + lax.broadcasted_iota(jnp.int32, (rg, wb + PAGE), 1)
        qpos = past + tcol
        vis = (kpos <= qpos) & (kpos > qpos - WINDOW)
        wst = wst_ref[0]
        gates = gate_ref[0]
        ocmp = ocmp_ref[0]
        outs = []
        for g in range(g_n):
            qg = qs_ref[g * rg:(g + 1) * rg, :].astype(BF16)
            kw = jnp.concatenate([wst[:, g * HEAD_DIM:(g + 1) * HEAD_DIM], new_keys(kwn_ref, g)], axis=0)
            vw = jnp.concatenate([wst[:, NSA_KVW + g * HEAD_DIM:NSA_KVW + (g + 1) * HEAD_DIM],
                                  new_keys(vwn_ref, g)], axis=0)
            o_win = _softmax_once(_dot_nt(qg, kw.astype(BF16)), vis, vw.astype(BF16))
            gt = jax.nn.sigmoid(gates[:, g * V7X_LANES:(g + 1) * V7X_LANES])
            for h in range(r):
                rows = slice(g * rg + h * t, g * rg + (h + 1) * t)
                outs.append(gt[:, 3 * h:3 * h + 1] * ocmp[rows] + gt[:, 3 * h + 1:3 * h + 2] * o_slc[rows]
                            + gt[:, 3 * h + 2:3 * h + 3] * o_win[h * t:(h + 1) * t])
        tok_ref[0] = jnp.concatenate(outs, axis=-1)


def _nsa_sample_attend(cache, pages, flags, proj, sel, ocmp, win_state, win_index0, past):
    b, t, _ = proj.shape
    npg = pages.shape[0] // b
    nb = sel.shape[2]
    wb = win_state.shape[1]
    assert nb * SLC_BLOCK == npg * PAGE == past and t <= SLC_BLOCK
    col = lambda first: first * V7X_LANES // NSA_KVW
    new_spec = lambda name: pl.BlockSpec((1, t, NSA_KVW), lambda bi, j, pg, fl: (bi, 0, col(NSA_COLS[name])))
    vmem = 2 * (PAGE * 2 * NSA_KVW * 4 + wb * 2 * NSA_KVW * 4 + t * NSA_PROJ_W * 4) + 8 * (wb + PAGE) * 128 * 4
    grid_spec = pltpu.PrefetchScalarGridSpec(
        num_scalar_prefetch=2,
        grid=(b, npg),
        in_specs=[
            pl.BlockSpec((1, PAGE, 2 * NSA_KVW), lambda bi, j, pg, fl: (pg[bi * npg + j], 0, 1)),
            pl.BlockSpec((1, t, NSA_Q_W), lambda bi, j, pg, fl: (bi, 0, 0)),
            pl.BlockSpec((1, NSA_KV * t, nb), lambda bi, j, pg, fl: (bi, 0, 0)),
            pl.BlockSpec((1, NSA_HEADS * t, HEAD_DIM), lambda bi, j, pg, fl: (bi, 0, 0)),
            new_spec("ks"), new_spec("vs"), new_spec("kw"), new_spec("vw"),
            pl.BlockSpec((1, wb, 2 * NSA_KVW), lambda bi, j, pg, fl: (win_index0 + bi, 0, 0)),
            pl.BlockSpec((1, t, NSA_KV * V7X_LANES), lambda bi, j, pg, fl: (bi, 0, col(NSA_COLS["gates"]))),
        ],
        out_specs=pl.BlockSpec((1, t, NSA_Q_W), lambda bi, j, pg, fl: (bi, 0, 0)),
        scratch_shapes=[pltpu.VMEM((NSA_HEADS * t, HEAD_DIM), F32), pltpu.VMEM((NSA_HEADS * t, 1), F32),
                        pltpu.VMEM((NSA_HEADS * t, 1), F32), pltpu.VMEM((NSA_HEADS * t, HEAD_DIM), F32)],
    )
    return pl.pallas_call(
        functools.partial(_nsa_sample_attend_kernel, past=past, npg=npg),
        grid_spec=grid_spec,
        out_shape=jax.ShapeDtypeStruct((b, t, NSA_Q_W), F32),
        compiler_params=_cparams(2, vmem),
        name="nsa_sample_attend",
    )(pages, flags, cache, proj, sel, ocmp, proj, proj, proj, proj, win_state, proj)


def _nsa_sample(proj, cache, page_ids, kc, vc, win_state, win_index0, past):
    b, t, _ = proj.shape
    npg = page_ids.shape[1]
    ocmp, pslc = _nsa_sample_cmp(proj, kc, vc, past)
    nb = pslc.shape[2]
    qpos = jnp.broadcast_to(past + jnp.arange(t, dtype=jnp.int32), (b * NSA_KV, t)).reshape(1, -1)
    sel_t = _rank_blocks(pslc.reshape(b * NSA_KV * t, nb).T, qpos, TOP_N - 1)
    sel = sel_t.T.reshape(b, NSA_KV * t, nb)
    page_hit = sel.reshape(b, NSA_KV * t, npg, nb // npg).max(axis=(1, 3)) > 0
    last_hit = jnp.maximum(lax.cummax(jnp.where(page_hit, jnp.arange(npg, dtype=jnp.int32), -1), axis=1), 0)
    pages = jnp.take_along_axis(page_ids, last_hit, axis=1).reshape(-1)
    return _nsa_sample_attend(cache, pages, page_hit.astype(jnp.int32).reshape(-1), proj, sel.astype(BF16), ocmp,
                              win_state, win_index0, past)


PROMPT_TM = 512
PROJ_TN = 512
FFN_TM = 512
FFN_TF = 512
WO_TN = 1024
MEM_TQ = 512
NSA_TQ = 128
NSA_TK = 256
CMP_PAGES = 16


def kernel(x_prompt, x_sample, cache_nsa_kv, state_nsa_win, state_ret, state_ffn_conv, cache_mem_kv, page_table,
           mem_prompt, norm1_g, nsa_w_in, nsa_cmp_pe, nsa_cmp_w1, nsa_cmp_w2, ret_w_in, ret_gn_g, mem_norm_g,
           w_mem_kv, w_o, norm2_g, ffn_w_in, ffn_conv_w, ffn_conv_b, ffn_w_out, final_norm_g):
    bp, s_len, d = x_prompt.shape
    db, t_len, _ = x_sample.shape
    depth = norm1_g.shape[0]
    n_phys = cache_nsa_kv.shape[1]
    n_mem = mem_prompt.shape[1]
    fdim = ffn_w_out.shape[1]
    ret_heads = state_ret.shape[2]
    tok_w = ret_heads * RET_HEAD_DIM
    past = page_table.shape[1] * PAGE
    wb = state_nsa_win.shape[2]
    keep_p = min(WINDOW, s_len)
    assert cache_nsa_kv.shape[2] == PAGE and tok_w == NSA_Q_W and wb == WINDOW
    assert s_len % RET_CHUNK == 0 and t_len <= RET_CHUNK and t_len % 8 == 0

    xp = x_prompt.reshape(bp * s_len, d)
    xs = x_sample.reshape(db * t_len, d)
    cache = cache_nsa_kv.reshape(-1, PAGE, 4 * NSA_KVW)
    win_state = state_nsa_win.reshape(-1, wb, 2 * NSA_KVW)
    mem2 = mem_prompt.reshape(bp * n_mem, d)
    prompt_pages = jnp.arange(bp * s_len // PAGE, dtype=jnp.int32)
    pos_p = jnp.arange(s_len, dtype=jnp.int32)
    pos_s = past + jnp.minimum(jnp.arange(RET_CHUNK, dtype=jnp.int32), t_len - 1)
    ret_tab_p = _retention_tables(ret_heads, RET_CHUNK, RET_CHUNK, pos_p)
    ret_tab_s = _retention_tables(ret_heads, RET_CHUNK, t_len, pos_s)
    zero_state = jnp.zeros((bp, ret_heads, RET_HEAD_DIM, RET_HEAD_DIM), F32)
    zero_conv = jnp.zeros((bp, CONV_W - 1, fdim), F32)

    kv_p, kv_s, win_p, win_s, ret_p, ret_s, conv_p, conv_s, mem_p = ([] for _ in range(9))
    for i in range(depth):
        mem_kv_p = _norm_matmul(mem2, mem_norm_g[i], w_mem_kv[i].astype(BF16), n_mem, PROJ_TN)
        mem_kv_p = mem_kv_p.reshape(bp, n_mem, 2 * MEM_W)
        mem_p.append(mem_kv_p.reshape(bp, n_mem, 2, MEM_HEADS, HEAD_DIM))
        mem_kv_s = cache_mem_kv[i].reshape(db, n_mem, 2 * MEM_W)
        if i % 2 == 0:
            a = i // 2
            w_in = _prep_nsa_w(nsa_w_in[a])
            pe, w1, w2 = nsa_cmp_pe[a], nsa_cmp_w1[a].astype(BF16), nsa_cmp_w2[a].astype(BF16)
            proj_p = _norm_matmul(xp, norm1_g[i], w_in, PROMPT_TM, PROJ_TN)
            proj_s = _norm_matmul(xs, norm1_g[i], w_in, db * t_len, PROJ_TN)
            p3 = proj_p.reshape(bp, s_len, NSA_PROJ_W)
            s3 = proj_s.reshape(db, t_len, NSA_PROJ_W)
            cmp = _even_odd(_compress(proj_p.reshape(-1, PAGE, NSA_PROJ_W), prompt_pages, NSA_Q_W, pe, w1, w2,
                                      CMP_PAGES), bp)
            tok_p = _nsa_prompt(p3, cmp[0], cmp[1], NSA_COLS, NSA_TQ, NSA_TK)
            page_ids = page_table + a * n_phys
            cmp = _even_odd(_compress(cache, page_ids.reshape(-1), 0, pe, w1, w2, CMP_PAGES), db)
            tok_s = _nsa_sample(s3, cache, page_ids, cmp[0], cmp[1], win_state, a * db, past)
            qm_block = NSA_MAIN_W // MEM_W
            rows_lo, rows_hi = NSA_Q_W, NSA_Q_W + NSA_ROWS_W
            kv_p.append(p3[:, :, rows_lo:rows_hi].reshape(bp, s_len, 4, NSA_KV, HEAD_DIM))
            kv_s.append(s3[:, :, rows_lo:rows_hi].reshape(db, t_len, 4, NSA_KV, HEAD_DIM))
            win_p.append(p3[:, s_len - keep_p:, rows_hi:NSA_MAIN_W].reshape(bp, keep_p, 2, NSA_KV, HEAD_DIM))
            new_win = s3[:, :, rows_hi:NSA_MAIN_W].reshape(db, t_len, 2, NSA_KV, HEAD_DIM)
            win_s.append(jnp.concatenate([state_nsa_win[a], new_win], axis=1)[:, -wb:])
        else:
            bl = i // 2
            w_in = ret_w_in[bl].astype(BF16)
            proj_p = _norm_matmul(xp, norm1_g[i], w_in, PROMPT_TM, PROJ_TN)
            proj_s = _norm_matmul(xs, norm1_g[i], w_in, db * t_len, PROJ_TN)
            p3 = proj_p.reshape(bp, s_len, -1)
            s3 = proj_s.reshape(db, t_len, -1)
            tok_p, sp = _retention(p3, ret_gn_g[bl], zero_state, *ret_tab_p)
            s3_pad = jnp.pad(s3, ((0, 0), (0, RET_CHUNK - t_len), (0, 0)))
            tok_s, ss = _retention(s3_pad, ret_gn_g[bl], state_ret[bl], *ret_tab_s)
            tok_s = tok_s[:, :t_len]
            ret_p.append(sp)
            ret_s.append(ss)
            qm_block = 4 * tok_w // MEM_W
        mem_out_p = _mem_attend(p3, qm_block, mem_kv_p, MEM_TQ)
        mem_out_s = _mem_attend(s3, qm_block, mem_kv_s, t_len)
        wo = w_o[i].astype(BF16)
        xp = _wo(xp, tok_p.reshape(bp * s_len, tok_w), mem_out_p.reshape(bp * s_len, MEM_W), wo[:tok_w], wo[tok_w:],
                 PROMPT_TM, WO_TN)
        xs = _wo(xs, tok_s.reshape(db * t_len, tok_w), mem_out_s.reshape(db * t_len, MEM_W), wo[:tok_w], wo[tok_w:],
                 db * t_len, WO_TN)
        last = i == depth - 1
        ffn_w = (norm2_g[i], ffn_w_in[i].astype(BF16), ffn_conv_w[i], ffn_conv_b[i], ffn_w_out[i].astype(BF16))
        xp, tails = _ffn(xp, *ffn_w, zero_conv, final_norm_g, seq_len=s_len, tm=FFN_TM, tf=FFN_TF, final_norm=last)
        conv_p.append(tails[s_len // FFN_TM - 1::s_len // FFN_TM])
        xs, tails = _ffn(xs, *ffn_w, state_ffn_conv[i], final_norm_g, seq_len=t_len, tm=t_len, tf=FFN_TF,
                         final_norm=last)
        conv_s.append(tails)
    return (xp.reshape(bp, s_len, d), xs.reshape(db, t_len, d), jnp.stack(kv_p), jnp.stack(kv_s), jnp.stack(win_p),
            jnp.stack(win_s), jnp.stack(ret_p), jnp.stack(ret_s), jnp.stack(conv_p), jnp.stack(conv_s),
            jnp.stack(mem_p))
```

```python
import functools

import jax
import jax.numpy as jnp
import numpy as np
from jax import lax
from jax.experimental import pallas as pl
from jax.experimental.pallas import tpu as pltpu

F32 = jnp.float32
BF16 = jnp.bfloat16

HEAD_DIM = 128
MEM_HEADS = 4
NSA_KV = 4
CMP_BLOCK = 32
SLC_BLOCK = 64
TOP_N = 16
WINDOW = 512
FORCE_BONUS = 1.0e4
RET_HEAD_DIM = 256
RET_CHUNK = 128
ROPE_BASE = 10000.0
CONV_W = 3
NORM_EPS = 1e-6
PAGE = 128

V7X_LANES = 128
V7X_VMEM_BYTES = 64 * 1024 * 1024
NEG_BIG = -1e30


def _cparams(n_axes, vmem_bytes):
    limit = min(int(vmem_bytes * 1.25) + (8 << 20), V7X_VMEM_BYTES - (4 << 20))
    return pltpu.CompilerParams(dimension_semantics=("arbitrary",) * n_axes, vmem_limit_bytes=limit)


def _dot(a, b):
    return jnp.dot(a, b, preferred_element_type=F32)


def _dot_nt(a, b):
    return lax.dot_general(a, b, (((1,), (1,)), ((), ())), preferred_element_type=F32)


def _rms(x, g):
    ms = jnp.mean(x * x, axis=-1, keepdims=True)
    return x * lax.rsqrt(ms + NORM_EPS) * g


def _norm_matmul_kernel(x_ref, g_ref, w_ref, o_ref, hn_ref):
    @pl.when(pl.program_id(1) == 0)
    def _():
        hn_ref[...] = _rms(x_ref[...], g_ref[...]).astype(BF16)

    o_ref[...] = _dot(hn_ref[...], w_ref[...])


def _norm_matmul(x, g, w, tm, tn):
    m, k = x.shape
    n = w.shape[1]
    vmem = 2 * tm * k * 4 + tm * k * 2 + 2 * k * tn * 2 + 2 * tm * tn * 4
    return pl.pallas_call(
        _norm_matmul_kernel,
        grid=(m // tm, n // tn),
        in_specs=[
            pl.BlockSpec((tm, k), lambda i, j: (i, 0)),
            pl.BlockSpec((1, k), lambda i, j: (0, 0)),
            pl.BlockSpec((k, tn), lambda i, j: (0, j)),
        ],
        out_specs=pl.BlockSpec((tm, tn), lambda i, j: (i, j)),
        out_shape=jax.ShapeDtypeStruct((m, n), F32),
        scratch_shapes=[pltpu.VMEM((tm, k), BF16)],
        compiler_params=_cparams(2, vmem),
        name="norm_matmul",
    )(x, g.reshape(1, k), w)


def _wo_kernel(x_ref, tok_ref, mem_ref, wt_ref, wm_ref, o_ref):
    o_ref[...] = (x_ref[...] + _dot(tok_ref[...].astype(BF16), wt_ref[...])
                  + _dot(mem_ref[...].astype(BF16), wm_ref[...]))


def _wo(x, tok, mem, w_tok, w_mem, tm, tn):
    m, d = x.shape
    kt, km = tok.shape[1], mem.shape[1]
    vmem = 2 * (tm * tn * 8 + tm * (kt + km) * 4 + (kt + km) * tn * 2)
    return pl.pallas_call(
        _wo_kernel,
        grid=(m // tm, d // tn),
        in_specs=[
            pl.BlockSpec((tm, tn), lambda i, j: (i, j)),
            pl.BlockSpec((tm, kt), lambda i, j: (i, 0)),
            pl.BlockSpec((tm, km), lambda i, j: (i, 0)),
            pl.BlockSpec((kt, tn), lambda i, j: (0, j)),
            pl.BlockSpec((km, tn), lambda i, j: (0, j)),
        ],
        out_specs=pl.BlockSpec((tm, tn), lambda i, j: (i, j)),
        out_shape=jax.ShapeDtypeStruct((m, d), F32),
        compiler_params=_cparams(2, vmem),
        name="wo_residual",
    )(x, tok, mem, w_tok, w_mem)


def _ffn_kernel(x_ref, g_ref, wa_ref, wg_ref, cw_ref, cb_ref, wo_ref, buf_ref, fg_ref,
                y_ref, tail_ref, hn_ref, carry_ref, *, tiles_per_seq, n_f, final_norm):
    i = pl.program_id(0)
    f = pl.program_id(1)
    tm = x_ref.shape[0]

    @pl.when(f == 0)
    def _():
        hn_ref[...] = _rms(x_ref[...], g_ref[...]).astype(BF16)

    hn = hn_ref[...]
    a = _dot(hn, wa_ref[...])
    gv = _dot(hn, wg_ref[...])
    seq_start = (i % tiles_per_seq) == 0
    prev = jnp.where(seq_start, buf_ref[0], carry_ref[f])
    row = lax.broadcasted_iota(jnp.int32, a.shape, 0)
    a1 = jnp.where(row == 0, prev[1:2], pltpu.roll(a, 1, axis=0))
    a2 = jnp.where(row == 0, prev[0:1], jnp.where(row == 1, prev[1:2], pltpu.roll(a, 2, axis=0)))
    cw = cw_ref[...]
    ac = cb_ref[...] + a2 * cw[0:1] + a1 * cw[1:2] + a * cw[2:3]
    act = (ac * jax.nn.sigmoid(ac) * gv).astype(BF16)
    last2 = a[tm - 2:tm]
    carry_ref[f] = last2
    tail_ref[0] = last2
    contrib = _dot(act, wo_ref[...])

    @pl.when(f == 0)
    def _():
        y_ref[...] = x_ref[...] + contrib

    @pl.when(f != 0)
    def _():
        y_ref[...] += contrib

    if final_norm:
        @pl.when(f == n_f - 1)
        def _():
            y_ref[...] = _rms(y_ref[...], fg_ref[...])


def _ffn(x, norm_g, w_in, conv_w, conv_b, w_out, buf, final_g, *, seq_len, tm, tf, final_norm):
    m, d = x.shape
    fdim = w_out.shape[0]
    nf = fdim // tf
    tiles_per_seq = seq_len // tm
    vmem = (3 * tm * d * 4 + tm * d * 2 + 2 * (2 * d * tf * 2 + tf * d * 2) + 8 * tm * tf * 4
            + nf * 8 * tf * 4)
    kern = functools.partial(_ffn_kernel, tiles_per_seq=tiles_per_seq, n_f=nf, final_norm=final_norm)
    return pl.pallas_call(
        kern,
        grid=(m // tm, nf),
        in_specs=[
            pl.BlockSpec((tm, d), lambda i, f: (i, 0), pipeline_mode=pl.Buffered(1)),
            pl.BlockSpec((1, d), lambda i, f: (0, 0)),
            pl.BlockSpec((d, tf), lambda i, f: (0, f)),
            pl.BlockSpec((d, tf), lambda i, f: (0, f + nf)),
            pl.BlockSpec((CONV_W, tf), lambda i, f: (0, f)),
            pl.BlockSpec((1, tf), lambda i, f: (0, f)),
            pl.BlockSpec((tf, d), lambda i, f: (f, 0)),
            pl.BlockSpec((1, CONV_W - 1, tf), lambda i, f: (i // tiles_per_seq, 0, f)),
            pl.BlockSpec((1, d), lambda i, f: (0, 0)),
        ],
        out_specs=[
            pl.BlockSpec((tm, d), lambda i, f: (i, 0), pipeline_mode=pl.Buffered(1)),
            pl.BlockSpec((1, CONV_W - 1, tf), lambda i, f: (i, 0, f)),
        ],
        out_shape=[jax.ShapeDtypeStruct((m, d), F32),
                   jax.ShapeDtypeStruct((m // tm, CONV_W - 1, fdim), F32)],
        scratch_shapes=[pltpu.VMEM((tm, d), BF16), pltpu.VMEM((nf, CONV_W - 1, tf), F32)],
        compiler_params=_cparams(2, vmem),
        name="conv_ffn",
    )(x, norm_g.reshape(1, d), w_in, w_in, conv_w, conv_b.reshape(1, fdim), w_out, buf,
      final_g.reshape(1, d))


def _mem_kernel(q_ref, kv_ref, o_ref):
    q = q_ref[0] * (HEAD_DIM ** -0.5)
    kv = kv_ref[0]
    width = MEM_HEADS * HEAD_DIM
    outs = []
    for h in range(MEM_HEADS):
        lo = h * HEAD_DIM
        qh = q[:, lo:lo + HEAD_DIM].astype(BF16)
        kh = kv[:, lo:lo + HEAD_DIM].astype(BF16)
        vh = kv[:, width + lo:width + lo + HEAD_DIM].astype(BF16)
        s = _dot_nt(qh, kh)
        e = jnp.exp(s - jnp.max(s, axis=-1, keepdims=True))
        p = e / jnp.sum(e, axis=-1, keepdims=True)
        outs.append(_dot(p.astype(BF16), vh))
    o_ref[0] = jnp.concatenate(outs, axis=-1)


def _mem_attend(proj, qm_col_block, mem_kv, tq):
    b, t, _ = proj.shape
    width = MEM_HEADS * HEAD_DIM
    n_mem = mem_kv.shape[1]
    vmem = 2 * (2 * tq * width * 4 + n_mem * 2 * width * 4) + 8 * tq * n_mem * 4
    return pl.pallas_call(
        _mem_kernel,
        grid=(b, t // tq),
        in_specs=[
            pl.BlockSpec((1, tq, width), lambda bi, i: (bi, i, qm_col_block)),
            pl.BlockSpec((1, n_mem, 2 * width), lambda bi, i: (bi, 0, 0)),
        ],
        out_specs=pl.BlockSpec((1, tq, width), lambda bi, i: (bi, i, 0)),
        out_shape=jax.ShapeDtypeStruct((b, t, width), F32),
        compiler_params=_cparams(2, vmem),
        name="mem_attend",
    )(proj, mem_kv)


def _ret_kernel(q_ref, k_ref, v_ref, gate_ref, cos_ref, sin_ref, dmat_ref, qd_ref, kd_ref, cd_ref,
                gn_ref, s0_ref, tok_ref, s_out_ref, s_ref, *, n_chunks):
    c = pl.program_id(2)

    @pl.when(c == 0)
    def _():
        s_ref[...] = s0_ref[0, 0]

    half = RET_HEAD_DIM // 2
    cos = cos_ref[...]
    sin = sin_ref[...]

    def rot(x):
        x1, x2 = x[:, :half], x[:, half:]
        return jnp.concatenate([x1 * cos - x2 * sin, x1 * sin + x2 * cos], axis=-1)

    q = rot(q_ref[0])
    k = rot(k_ref[0]) * (RET_HEAD_DIM ** -0.5)
    v = v_ref[0].astype(BF16)
    qb = q.astype(BF16)
    s_old = s_ref[...]
    inner = _dot_nt(qb, k.astype(BF16)) * dmat_ref[0]
    o = _dot(inner.astype(BF16), v) + _dot(qb, s_old.astype(BF16)) * qd_ref[0]
    kd = (k * kd_ref[0]).astype(BF16)
    s_new = s_old * cd_ref[0] + _dot(kd.T, v)
    s_ref[...] = s_new

    mu = jnp.mean(o, axis=-1, keepdims=True)
    dev = o - mu
    var = jnp.mean(dev * dev, axis=-1, keepdims=True)
    y = dev * lax.rsqrt(var + NORM_EPS) * gn_ref[0]
    gate = gate_ref[0]
    tok_ref[0] = gate * jax.nn.sigmoid(gate) * y

    @pl.when(c == n_chunks - 1)
    def _():
        s_out_ref[0, 0] = s_new


def _retention(proj, gn_g, s0, cos, sin, dmat, q_decay, k_decay, c_decay):
    b, t, _ = proj.shape
    h = s0.shape[1]
    dk = RET_HEAD_DIM
    c = dmat.shape[1]
    n = t // c
    vmem = 2 * (5 * c * dk * 4 + c * dk * 4 + c * c * 4 + 2 * c * 128 * 4 + 2 * dk * dk * 4) + 3 * dk * dk * 4
    blk = lambda off: pl.BlockSpec((1, c, dk), lambda bi, hi, ci: (bi, ci, off * h + hi))
    return pl.pallas_call(
        functools.partial(_ret_kernel, n_chunks=n),
        grid=(b, h, n),
        in_specs=[
            blk(0), blk(1), blk(2), blk(3),
            pl.BlockSpec((c, dk // 2), lambda bi, hi, ci: (ci, 0)),
            pl.BlockSpec((c, dk // 2), lambda bi, hi, ci: (ci, 0)),
            pl.BlockSpec((1, c, c), lambda bi, hi, ci: (hi, 0, 0)),
            pl.BlockSpec((1, c, 1), lambda bi, hi, ci: (hi, 0, 0)),
            pl.BlockSpec((1, c, 1), lambda bi, hi, ci: (hi, 0, 0)),
            pl.BlockSpec((1, 1, dk), lambda bi, hi, ci: (hi, 0, 0)),
            pl.BlockSpec((1, 1, dk), lambda bi, hi, ci: (hi, 0, 0)),
            pl.BlockSpec((1, 1, dk, dk), lambda bi, hi, ci: (bi, hi, 0, 0)),
        ],
        out_specs=[
            pl.BlockSpec((1, c, dk), lambda bi, hi, ci: (bi, ci, hi)),
            pl.BlockSpec((1, 1, dk, dk), lambda bi, hi, ci: (bi, hi, 0, 0)),
        ],
        out_shape=[jax.ShapeDtypeStruct((b, t, h * dk), F32),
                   jax.ShapeDtypeStruct((b, h, dk, dk), F32)],
        scratch_shapes=[pltpu.VMEM((dk, dk), F32)],
        compiler_params=_cparams(3, vmem),
        name="retention",
    )(proj, proj, proj, proj, cos, sin, dmat, q_decay, k_decay, c_decay, gn_g.reshape(h, 1, dk), s0)


def _retention_tables(n_heads, c_pad, c_real, pos):
    log_g = jnp.log1p(-jnp.exp2(-5.0 - jnp.arange(n_heads, dtype=F32)))
    i = jnp.arange(c_pad, dtype=F32)
    live = i < c_real
    diff = i[:, None] - i[None, :]
    dmat = jnp.where((diff >= 0) & live[:, None] & live[None, :],
                     jnp.exp(jnp.maximum(diff, 0.0)[None] * log_g[:, None, None]), 0.0)
    q_decay = jnp.exp((i + 1.0)[None, :] * log_g[:, None])[..., None]
    k_decay = jnp.where(live[None, :], jnp.exp((c_real - 1.0 - i)[None, :] * log_g[:, None]), 0.0)[..., None]
    c_decay = jnp.broadcast_to(jnp.exp(c_real * log_g)[:, None, None], (n_heads, 1, RET_HEAD_DIM))
    half = RET_HEAD_DIM // 2
    inv = ROPE_BASE ** (-jnp.arange(half, dtype=F32) / half)
    ang = pos.astype(F32)[:, None] * inv[None, :]
    return jnp.cos(ang), jnp.sin(ang), dmat, q_decay, k_decay, c_decay


KV_PLANES = 4 * NSA_KV
CMP_PLANES = 2 * NSA_KV


def _compress_kernel(plist_ref, src_ref, pe_ref, w1_ref, w2_ref, o_ref, buf_ref, lhs_ref, sem_ref, *, pages, n_steps):
    s = pl.program_id(0)
    slot = s % 2
    bpp = PAGE // CMP_BLOCK
    nblk = pages * bpp
    rows = nblk * CMP_PLANES

    def copies(step, to_slot):
        cps = []
        for p in range(pages):
            tok0 = plist_ref[step * pages + p] * PAGE
            for n in range(bpp):
                cps.append(pltpu.make_async_copy(
                    src_ref.at[pl.ds(tok0 + n * CMP_BLOCK, CMP_BLOCK), pl.ds(0, CMP_PLANES), :],
                    buf_ref.at[to_slot, :, p * bpp + n], sem_ref.at[to_slot]))
        return cps

    @pl.when(s == 0)
    def _():
        for cp in copies(0, 0):
            cp.start()

    @pl.when(s + 1 < n_steps)
    def _():
        for cp in copies(s + 1, 1 - slot):
            cp.start()

    for cp in copies(s, slot):
        cp.wait()

    for c in range(CMP_BLOCK):
        x = buf_ref[slot, c] + pe_ref[c]
        lhs_ref[:, c * HEAD_DIM:(c + 1) * HEAD_DIM] = x.reshape(rows, HEAD_DIM).astype(BF16)
    is_key = (lax.broadcasted_iota(jnp.int32, (rows, HEAD_DIM), 0) % CMP_PLANES) < NSA_KV
    h = _dot(lhs_ref[...], w1_ref[...])
    h = jax.nn.gelu(jnp.where(is_key, h[:, :HEAD_DIM], h[:, HEAD_DIM:]))
    out = _dot(h.astype(BF16), w2_ref[...])
    o_ref[...] = jnp.where(is_key, out[:, :HEAD_DIM], out[:, HEAD_DIM:]).reshape(nblk, CMP_PLANES, HEAD_DIM)


def _compress(src, plist, pe, w1, w2, pages):
    n_pages = plist.shape[0]
    nblk = pages * (PAGE // CMP_BLOCK)
    rows = nblk * CMP_PLANES
    k_dim = CMP_BLOCK * HEAD_DIM
    pe8 = jnp.repeat(pe.transpose(1, 0, 2), NSA_KV, axis=1)
    w1b = jnp.concatenate([w1[0], w1[1]], axis=1).astype(BF16)
    w2b = jnp.concatenate([w2[0], w2[1]], axis=1).astype(BF16)
    vmem = 2 * CMP_BLOCK * rows * HEAD_DIM * 4 + rows * k_dim * 2 + 2 * k_dim * 2 * HEAD_DIM * 2 + 8 * rows * HEAD_DIM * 4
    grid_spec = pltpu.PrefetchScalarGridSpec(
        num_scalar_prefetch=1,
        grid=(n_pages // pages,),
        in_specs=[
            pl.BlockSpec(memory_space=pl.ANY),
            pl.BlockSpec((CMP_BLOCK, CMP_PLANES, HEAD_DIM), lambda s, pr: (0, 0, 0)),
            pl.BlockSpec((k_dim, 2 * HEAD_DIM), lambda s, pr: (0, 0)),
            pl.BlockSpec((HEAD_DIM, 2 * HEAD_DIM), lambda s, pr: (0, 0)),
        ],
        out_specs=pl.BlockSpec((nblk, CMP_PLANES, HEAD_DIM), lambda s, pr: (s, 0, 0)),
        scratch_shapes=[pltpu.VMEM((2, CMP_BLOCK, nblk, CMP_PLANES, HEAD_DIM), F32),
                        pltpu.VMEM((rows, k_dim), BF16), pltpu.SemaphoreType.DMA((2,))],
    )
    return pl.pallas_call(
        functools.partial(_compress_kernel, pages=pages, n_steps=n_pages // pages),
        grid_spec=grid_spec,
        out_shape=jax.ShapeDtypeStruct((n_pages * (PAGE // CMP_BLOCK), CMP_PLANES, HEAD_DIM), F32),
        compiler_params=_cparams(1, vmem),
        name="nsa_compress",
    )(plist, src, pe8, w1b, w2b)


def _even_odd(c, n_seq):
    total, _, d = c.shape
    nc = total // n_seq
    c = c.reshape(n_seq, nc // 2, 2, 2, NSA_KV, d).transpose(3, 4, 0, 2, 1, 5)
    return c.reshape(2, NSA_KV, n_seq, nc, d).astype(BF16)


def _cmp_attend(qb, kc, vc, qpos_col):
    nc = kc.shape[0]
    lane = lax.broadcasted_iota(jnp.int32, (1, nc), 1)
    blk = jnp.where(lane < nc // 2, 2 * lane, 2 * (lane - nc // 2) + 1)
    vis = (blk * CMP_BLOCK + (CMP_BLOCK - 1)) <= qpos_col
    s = jnp.where(vis, _dot_nt(qb, kc), -jnp.inf)
    m = jnp.max(s, axis=-1, keepdims=True)
    m = jnp.where(m > -jnp.inf, m, 0.0)
    e = jnp.where(vis, jnp.exp(s - m), 0.0)
    p = e / jnp.maximum(jnp.sum(e, axis=-1, keepdims=True), 1e-30)
    return _dot(p.astype(BF16), vc), p


def _select_blocks(p_t, qpos_row, n_sel, m_hi, score_ref):
    nb, nq = p_t.shape
    blk = lax.broadcasted_iota(jnp.int32, (nb, nq), 0)
    valid = blk * SLC_BLOCK <= qpos_row
    cur = qpos_row // SLC_BLOCK
    forced = (blk == 0) | (blk == cur) | (blk == cur - 1)
    score = jnp.where(valid, p_t + jnp.where(forced, FORCE_BONUS, 0.0), -jnp.inf)
    score_ref[...] = score
    rows = min(64, nb)
    outs = []
    for r0 in range(0, nb, rows):
        sc = score[r0:r0 + rows]
        bk = blk[r0:r0 + rows]

        def body(mi, rank, sc=sc, bk=bk):
            other = score_ref[pl.ds(mi, 1), :]
            ge = jnp.where(other >= sc, 1.0, 0.0)
            gt = jnp.where(other > sc, 1.0, 0.0)
            return rank + jnp.where(bk > mi, ge, gt)

        rank = lax.fori_loop(0, m_hi, body, jnp.zeros_like(sc))
        outs.append(jnp.where((rank < n_sel) & valid[r0:r0 + rows], 1.0, 0.0))
    return jnp.concatenate(outs, axis=0) if len(outs) > 1 else outs[0]


def _lane_tile(x, n_lanes):
    reps = n_lanes // x.shape[1]
    return x if reps == 1 else jnp.concatenate([x] * reps, axis=1)


def _softmax_step(s, vis, v, m_ref, l_ref, acc_ref, r0):
    rows, n_keys = s.shape
    sl = pl.ds(r0, rows)
    s = jnp.where(vis, s, NEG_BIG)
    m_old = m_ref[sl, :]
    m_new = jnp.maximum(m_old, jnp.max(s, axis=-1, keepdims=True))
    alpha = jnp.exp(m_old - m_new)
    p = jnp.exp(s - _lane_tile(m_new, n_keys))
    l_ref[sl, :] = alpha * l_ref[sl, :] + jnp.sum(p, axis=-1, keepdims=True)
    acc_ref[sl, :] = alpha * acc_ref[sl, :] + _dot(p.astype(BF16), v)
    m_ref[sl, :] = m_new


def _softmax_finish(m_ref, l_ref, acc_ref):
    seen = m_ref[...] > 0.5 * NEG_BIG
    return jnp.where(seen, acc_ref[...] / jnp.maximum(l_ref[...], 1e-30), 0.0)


def _softmax_once(s, vis, v):
    s = jnp.where(vis, s, NEG_BIG)
    m = jnp.max(s, axis=-1, keepdims=True)
    e = jnp.where(vis, jnp.exp(s - m), 0.0)
    p = e / jnp.maximum(jnp.sum(e, axis=-1, keepdims=True), 1e-30)
    return _dot(p.astype(BF16), v)


def _block_expander(n_blocks, k0, n_keys):
    blk = lax.broadcasted_iota(jnp.int32, (n_blocks, n_keys), 0)
    key = k0 + lax.broadcasted_iota(jnp.int32, (n_blocks, n_keys), 1)
    return jnp.where(blk == key // SLC_BLOCK, 1.0, 0.0).astype(BF16)


def _nsa_prompt_kernel(q_ref, kc_ref, vc_ref, ks_ref, vs_ref, kw_ref, vw_ref, gate_ref, tok_ref,
                       score_ref, s_ref, m_ref, l_ref, acc_ref, *, tk):
    i = pl.program_id(2)
    tq = q_ref.shape[1]
    r = q_ref.shape[2] // HEAD_DIM
    nb = kc_ref.shape[2] // (SLC_BLOCK // CMP_BLOCK)
    nbp = -(-nb // V7X_LANES) * V7X_LANES
    t0 = i * tq
    q = q_ref[0] * (HEAD_DIM ** -0.5)
    qb = jnp.concatenate([q[:, h * HEAD_DIM:(h + 1) * HEAD_DIM] for h in range(r)], axis=0).astype(BF16)
    tcol = t0 + lax.broadcasted_iota(jnp.int32, (tq, 1), 0)
    trow = t0 + lax.broadcasted_iota(jnp.int32, (1, tq), 1)

    o_cmp, p = _cmp_attend(qb, kc_ref[0, 0], vc_ref[0, 0], jnp.concatenate([tcol] * r, axis=0))
    p_grp = p[0:tq]
    for h in range(1, r):
        p_grp = p_grp + p[h * tq:(h + 1) * tq]
    p_pair = p_grp + pltpu.roll(p_grp, nb, axis=1)
    m_hi = jnp.minimum((t0 + tq - 1) // SLC_BLOCK + 1, nb)
    sel_t = _select_blocks(p_pair.T[:nb], trow, TOP_N, m_hi, score_ref)
    if nbp > nb:
        sel_t = jnp.concatenate([sel_t, jnp.zeros((nbp - nb, tq), F32)], axis=0)
    sel = sel_t.T.astype(BF16)

    n_kt = (t0 + tq + tk - 1) // tk
    lane_groups = tk // V7X_LANES
    m_ref[...] = jnp.full(m_ref.shape, NEG_BIG, F32)

    def score_tile(j, carry):
        k0 = pl.multiple_of(j * tk, tk)
        s = _dot_nt(qb, ks_ref[0, pl.ds(k0, tk), :].astype(BF16))
        picked = _dot(sel, _block_expander(nbp, k0, tk))
        kpos = k0 + lax.broadcasted_iota(jnp.int32, (tq, tk), 1)
        vis = (picked > 0.5) & (kpos <= tcol)
        for h in range(r):
            rows = pl.ds(h * tq, tq)
            sh = jnp.where(vis, s[h * tq:(h + 1) * tq], NEG_BIG)
            s_ref[j, rows, :] = sh
            m = m_ref[rows, :]
            for c in range(lane_groups):
                m = jnp.maximum(m, sh[:, c * V7X_LANES:(c + 1) * V7X_LANES])
            m_ref[rows, :] = m
        return carry

    lax.fori_loop(0, n_kt, score_tile, 0)
    m_row = jnp.max(m_ref[...], axis=-1, keepdims=True)
    m_ref[...] = jnp.broadcast_to(m_row, m_ref.shape)
    l_ref[...] = jnp.zeros(l_ref.shape, F32)
    acc_ref[...] = jnp.zeros(acc_ref.shape, F32)

    def value_tile(j, carry):
        k0 = pl.multiple_of(j * tk, tk)
        p = jnp.exp(s_ref[j] - _lane_tile(m_ref[...], tk))
        l = l_ref[...]
        for c in range(lane_groups):
            l = l + p[:, c * V7X_LANES:(c + 1) * V7X_LANES]
        l_ref[...] = l
        acc_ref[...] += _dot(p.astype(BF16), vs_ref[0, pl.ds(k0, tk), :].astype(BF16))
        return carry

    lax.fori_loop(0, n_kt, value_tile, 0)
    denom = jnp.maximum(jnp.sum(l_ref[...], axis=-1, keepdims=True), 1e-30)
    o_slc = jnp.where(m_row > 0.5 * NEG_BIG, acc_ref[...] / denom, 0.0)

    span = WINDOW + tq
    w0 = pl.multiple_of(jnp.maximum(t0 + tq - span, 0), tq)
    kw = kw_ref[0, pl.ds(w0, span), :].astype(BF16)
    vw = vw_ref[0, pl.ds(w0, span), :].astype(BF16)
    s = _dot_nt(qb, kw)
    kpos = w0 + lax.broadcasted_iota(jnp.int32, (tq, span), 1)
    vis = (kpos <= tcol) & (kpos > tcol - WINDOW)
    o_win = [_softmax_once(s[h * tq:(h + 1) * tq], vis, vw) for h in range(r)]

    gt = jax.nn.sigmoid(gate_ref[0])
    outs = []
    for h in range(r):
        rows = slice(h * tq, (h + 1) * tq)
        outs.append(gt[:, 3 * h:3 * h + 1] * o_cmp[rows] + gt[:, 3 * h + 1:3 * h + 2] * o_slc[rows]
                    + gt[:, 3 * h + 2:3 * h + 3] * o_win[h])
    tok_ref[0] = jnp.concatenate(outs, axis=-1)


def _nsa_prompt(proj, kc, vc, cols, tq, tk):
    b, t, _ = proj.shape
    g = kc.shape[0]
    nc = kc.shape[2]
    r = 3
    nb = nc // (SLC_BLOCK // CMP_BLOCK)
    assert t >= WINDOW + tq and t % tk == 0 and tk % tq == 0 and nb % 64 == 0 and nc <= V7X_LANES
    kv_spec = lambda c0: pl.BlockSpec((1, t, HEAD_DIM), lambda bi, gi, i: (bi, 0, c0 + gi))
    cmp_spec = pl.BlockSpec((1, 1, nc, HEAD_DIM), lambda bi, gi, i: (gi, bi, 0, 0))
    vmem = 2 * 4 * t * HEAD_DIM * 4 + r * tq * t * 4 + 24 * r * tq * max(tk, WINDOW + tq) * 4
    return pl.pallas_call(
        functools.partial(_nsa_prompt_kernel, tk=tk),
        grid=(b, g, t // tq),
        in_specs=[
            pl.BlockSpec((1, tq, r * HEAD_DIM), lambda bi, gi, i: (bi, i, cols["q"] // r + gi)),
            cmp_spec, cmp_spec,
            kv_spec(cols["ks"]), kv_spec(cols["vs"]), kv_spec(cols["kw"]), kv_spec(cols["vw"]),
            pl.BlockSpec((1, tq, HEAD_DIM), lambda bi, gi, i: (bi, i, cols["gates"] + gi)),
        ],
        out_specs=pl.BlockSpec((1, tq, r * HEAD_DIM), lambda bi, gi, i: (bi, i, gi)),
        out_shape=jax.ShapeDtypeStruct((b, t, g * r * HEAD_DIM), F32),
        scratch_shapes=[pltpu.VMEM((nb, tq), F32), pltpu.VMEM((t // tk, r * tq, tk), F32),
                        pltpu.VMEM((r * tq, V7X_LANES), F32), pltpu.VMEM((r * tq, V7X_LANES), F32),
                        pltpu.VMEM((r * tq, HEAD_DIM), F32)],
        compiler_params=_cparams(3, vmem),
        name="nsa_prompt",
    )(proj, kc, vc, proj, proj, proj, proj, proj)


NSA_HEADS = 12
NSA_KVW = NSA_KV * HEAD_DIM
NSA_Q_W = NSA_HEADS * HEAD_DIM
NSA_ROWS_W = 4 * NSA_KVW
NSA_WIN_W = 2 * NSA_KVW
MEM_W = MEM_HEADS * HEAD_DIM
NSA_GATES = 3 * NSA_HEADS
NSA_MAIN_W = NSA_Q_W + NSA_ROWS_W + NSA_WIN_W
NSA_PROJ_W = NSA_MAIN_W + MEM_W + NSA_KV * V7X_LANES
NSA_COLS = {
    "q": 0,
    "ks": (NSA_Q_W + 2 * NSA_KVW) // V7X_LANES,
    "vs": (NSA_Q_W + 3 * NSA_KVW) // V7X_LANES,
    "kw": (NSA_Q_W + NSA_ROWS_W) // V7X_LANES,
    "vw": (NSA_Q_W + NSA_ROWS_W + NSA_KVW) // V7X_LANES,
    "gates": (NSA_MAIN_W + MEM_W) // V7X_LANES,
}


def _prep_nsa_w(w):
    d = w.shape[0]
    gates = w[:, NSA_MAIN_W:NSA_MAIN_W + NSA_GATES].reshape(d, NSA_KV, NSA_GATES // NSA_KV)
    gates = jnp.pad(gates, ((0, 0), (0, 0), (0, V7X_LANES - NSA_GATES // NSA_KV))).reshape(d, NSA_KV * V7X_LANES)
    return jnp.concatenate([w[:, :NSA_MAIN_W], w[:, NSA_MAIN_W + NSA_GATES:], gates], axis=1).astype(BF16)


def _stack_heads(q, n_heads):
    return jnp.concatenate([q[:, h * HEAD_DIM:(h + 1) * HEAD_DIM] for h in range(n_heads)], axis=0)


def _nsa_sample_cmp_kernel(q_ref, kc_ref, vc_ref, ocmp_ref, pslc_ref, *, past):
    t = q_ref.shape[1]
    g_n = kc_ref.shape[0]
    r = q_ref.shape[2] // HEAD_DIM // g_n
    nb = kc_ref.shape[2] // (SLC_BLOCK // CMP_BLOCK)
    q = q_ref[0] * (HEAD_DIM ** -0.5)
    qpos = past + lax.broadcasted_iota(jnp.int32, (t, 1), 0)
    qpos_r = jnp.concatenate([qpos] * r, axis=0)
    for g in range(g_n):
        qb = _stack_heads(q[:, g * r * HEAD_DIM:(g + 1) * r * HEAD_DIM], r).astype(BF16)
        o, p = _cmp_attend(qb, kc_ref[g, 0], vc_ref[g, 0], qpos_r)
        ocmp_ref[0, g * r * t:(g + 1) * r * t, :] = o
        p_grp = p[0:t]
        for h in range(1, r):
            p_grp = p_grp + p[h * t:(h + 1) * t]
        p_pair = p_grp + pltpu.roll(p_grp, nb, axis=1)
        pslc_ref[0, g * t:(g + 1) * t, :] = p_pair[:, :nb]


def _nsa_sample_cmp(proj, kc, vc, past):
    b, t, _ = proj.shape
    g, _, nc, _ = kc.shape
    nb = nc // (SLC_BLOCK // CMP_BLOCK)
    cmp_spec = pl.BlockSpec((g, 1, nc, HEAD_DIM), lambda bi: (0, bi, 0, 0))
    vmem = 2 * (t * NSA_Q_W * 4 + 2 * g * nc * HEAD_DIM * 2) + 64 * nc * 4 * 8
    return pl.pallas_call(
        functools.partial(_nsa_sample_cmp_kernel, past=past),
        grid=(b,),
        in_specs=[pl.BlockSpec((1, t, NSA_Q_W), lambda bi: (bi, 0, 0)), cmp_spec, cmp_spec],
        out_specs=[pl.BlockSpec((1, NSA_HEADS * t, HEAD_DIM), lambda bi: (bi, 0, 0)),
                   pl.BlockSpec((1, g * t, nb), lambda bi: (bi, 0, 0))],
        out_shape=[jax.ShapeDtypeStruct((b, NSA_HEADS * t, HEAD_DIM), F32),
                   jax.ShapeDtypeStruct((b, g * t, nb), F32)],
        compiler_params=_cparams(1, vmem),
        name="nsa_sample_cmp",
    )(proj, kc, vc)


def _rank_kernel(p_ref, qpos_ref, sel_ref, score_ref, *, n_sel):
    sel_ref[...] = _select_blocks(p_ref[...], qpos_ref[...], n_sel, p_ref.shape[0], score_ref)


def _rank_blocks(p_t, qpos, n_sel):
    nb, nq = p_t.shape
    tq = V7X_LANES
    return pl.pallas_call(
        functools.partial(_rank_kernel, n_sel=n_sel),
        grid=(nq // tq,),
        in_specs=[pl.BlockSpec((nb, tq), lambda i: (0, i)), pl.BlockSpec((1, tq), lambda i: (0, i))],
        out_specs=pl.BlockSpec((nb, tq), lambda i: (0, i)),
        out_shape=jax.ShapeDtypeStruct((nb, nq), F32),
        scratch_shapes=[pltpu.VMEM((nb, tq), F32)],
        compiler_params=_cparams(1, 8 * nb * tq * 4),
        name="nsa_rank",
    )(p_t, qpos)


def _nsa_sample_attend_kernel(pages_ref, *refs, past, n_steps, pps):
    kv_refs = refs[:pps]
    (q_ref, sel_ref, ocmp_ref, ksn_ref, vsn_ref, kwn_ref, vwn_ref, wst_ref, gate_ref, tok_ref,
     qs_ref, m_ref, l_ref, acc_ref) = refs[pps:]
    j = pl.program_id(1)
    t = q_ref.shape[1]
    g_n = NSA_KV
    r = NSA_HEADS // g_n
    rg = r * t
    nb = sel_ref.shape[2]
    span = pps * PAGE

    @pl.when(j == 0)
    def _():
        qs_ref[...] = _stack_heads(q_ref[0] * (HEAD_DIM ** -0.5), NSA_HEADS)
        m_ref[...] = jnp.full(m_ref.shape, NEG_BIG, F32)
        l_ref[...] = jnp.zeros(l_ref.shape, F32)
        acc_ref[...] = jnp.zeros(acc_ref.shape, F32)

    picked = _dot(sel_ref[0], _block_expander(nb, j * span, span))
    for g in range(g_n):
        kg = jnp.concatenate([kv[:, g, :] for kv in kv_refs], axis=0).astype(BF16)
        vg = jnp.concatenate([kv[:, g_n + g, :] for kv in kv_refs], axis=0).astype(BF16)
        s = _dot_nt(qs_ref[g * rg:(g + 1) * rg, :].astype(BF16), kg)
        vis = jnp.concatenate([picked[g * t:(g + 1) * t]] * r, axis=0) > 0.5
        _softmax_step(s, vis, vg, m_ref, l_ref, acc_ref, g * rg)

    @pl.when(j == n_steps - 1)
    def _():
        tcol = jnp.concatenate([lax.broadcasted_iota(jnp.int32, (t, 1), 0)] * r, axis=0)
        pad = jnp.zeros((PAGE - t, HEAD_DIM), F32)

        def new_keys(ref, g):
            return jnp.concatenate([ref[0][:, g * HEAD_DIM:(g + 1) * HEAD_DIM], pad], axis=0)

        lane = lax.broadcasted_iota(jnp.int32, (rg, PAGE), 1)
        for g in range(g_n):
            qg = qs_ref[g * rg:(g + 1) * rg, :].astype(BF16)
            _softmax_step(_dot_nt(qg, new_keys(ksn_ref, g).astype(BF16)), lane <= tcol,
                          new_keys(vsn_ref, g).astype(BF16), m_ref, l_ref, acc_ref, g * rg)
        o_slc = _softmax_finish(m_ref, l_ref, acc_ref)

        wb = wst_ref.shape[1]
        kpos = past - wb + lax.broadcasted_iota(jnp.int32, (rg, wb + PAGE), 1)
        qpos = past + tcol
        vis = (kpos <= qpos) & (kpos > qpos - WINDOW)
        wst = wst_ref[0]
        gates = gate_ref[0]
        ocmp = ocmp_ref[0]
        outs = []
        for g in range(g_n):
            qg = qs_ref[g * rg:(g + 1) * rg, :].astype(BF16)
            kw = jnp.concatenate([wst[:, g * HEAD_DIM:(g + 1) * HEAD_DIM], new_keys(kwn_ref, g)], axis=0)
            vw = jnp.concatenate([wst[:, NSA_KVW + g * HEAD_DIM:NSA_KVW + (g + 1) * HEAD_DIM],
                                  new_keys(vwn_ref, g)], axis=0)
            o_win = _softmax_once(_dot_nt(qg, kw.astype(BF16)), vis, vw.astype(BF16))
            gt = jax.nn.sigmoid(gates[:, g * V7X_LANES:(g + 1) * V7X_LANES])
            for h in range(r):
                rows = slice(g * rg + h * t, g * rg + (h + 1) * t)
                outs.append(gt[:, 3 * h:3 * h + 1] * ocmp[rows] + gt[:, 3 * h + 1:3 * h + 2] * o_slc[rows]
                            + gt[:, 3 * h + 2:3 * h + 3] * o_win[h * t:(h + 1) * t])
        tok_ref[0] = jnp.concatenate(outs, axis=-1)


def _nsa_sample_attend(cache, pages, proj, sel, ocmp, win_state, win_index0, past, pps):
    b, t, _ = proj.shape
    npg = pages.shape[0] // b
    nb = sel.shape[2]
    wb = win_state.shape[1]
    n_steps = npg // pps
    assert nb * SLC_BLOCK == npg * PAGE == past and t <= SLC_BLOCK and npg % pps == 0
    col = lambda first: first * V7X_LANES // NSA_KVW
    new_spec = lambda name: pl.BlockSpec((1, t, NSA_KVW), lambda bi, j, pg: (bi, 0, col(NSA_COLS[name])))
    page_spec = lambda k: pl.BlockSpec((PAGE, KV_PLANES // 2, HEAD_DIM),
                                       lambda bi, j, pg: (pg[bi * npg + j * pps + k], 1, 0))
    rows = NSA_HEADS * t
    vmem = (2 * (pps * PAGE * NSA_KVW * 2 * 4 + wb * 2 * NSA_KVW * 4 + t * NSA_PROJ_W * 4)
            + 6 * pps * PAGE * HEAD_DIM * 4 + 8 * (wb + PAGE) * 128 * 4)
    grid_spec = pltpu.PrefetchScalarGridSpec(
        num_scalar_prefetch=1,
        grid=(b, n_steps),
        in_specs=[page_spec(k) for k in range(pps)] + [
            pl.BlockSpec((1, t, NSA_Q_W), lambda bi, j, pg: (bi, 0, 0)),
            pl.BlockSpec((1, NSA_KV * t, nb), lambda bi, j, pg: (bi, 0, 0)),
            pl.BlockSpec((1, rows, HEAD_DIM), lambda bi, j, pg: (bi, 0, 0)),
            new_spec("ks"), new_spec("vs"), new_spec("kw"), new_spec("vw"),
            pl.BlockSpec((1, wb, 2 * NSA_KVW), lambda bi, j, pg: (win_index0 + bi, 0, 0)),
            pl.BlockSpec((1, t, NSA_KV * V7X_LANES), lambda bi, j, pg: (bi, 0, col(NSA_COLS["gates"]))),
        ],
        out_specs=pl.BlockSpec((1, t, NSA_Q_W), lambda bi, j, pg: (bi, 0, 0)),
        scratch_shapes=[pltpu.VMEM((rows, HEAD_DIM), F32), pltpu.VMEM((rows, V7X_LANES), F32),
                        pltpu.VMEM((rows, V7X_LANES), F32), pltpu.VMEM((rows, HEAD_DIM), F32)],
    )
    return pl.pallas_call(
        functools.partial(_nsa_sample_attend_kernel, past=past, n_steps=n_steps, pps=pps),
        grid_spec=grid_spec,
        out_shape=jax.ShapeDtypeStruct((b, t, NSA_Q_W), F32),
        compiler_params=_cparams(2, vmem),
        name="nsa_sample_attend",
    )(pages, *([cache] * pps), proj, sel, ocmp, proj, proj, proj, proj, win_state, proj)


def _nsa_sample(proj, cache, page_ids, kc, vc, win_state, win_index0, past, pps):
    b, t, _ = proj.shape
    ocmp, pslc = _nsa_sample_cmp(proj, kc, vc, past)
    nb = pslc.shape[2]
    qpos = jnp.broadcast_to(past + jnp.arange(t, dtype=jnp.int32), (b * NSA_KV, t)).reshape(1, -1)
    sel_t = _rank_blocks(pslc.reshape(b * NSA_KV * t, nb).T, qpos, TOP_N - 1)
    sel = sel_t.T.reshape(b, NSA_KV * t, nb).astype(BF16)
    return _nsa_sample_attend(cache, page_ids.reshape(-1), proj, sel, ocmp, win_state, win_index0, past, pps)


PROJ_TM = 1024
PROJ_TN = 512
WO_TM = 512
FFN_TM = 1024
FFN_TF = 256
WO_TN = 1024
MEM_TQ = 512
NSA_TQ = 128
NSA_TK = 256
CMP_PAGES = 16
SAMPLE_PAGES = 8


def kernel(x_prompt, x_sample, cache_nsa_kv, state_nsa_win, state_ret, state_ffn_conv, cache_mem_kv, page_table,
           mem_prompt, norm1_g, nsa_w_in, nsa_cmp_pe, nsa_cmp_w1, nsa_cmp_w2, ret_w_in, ret_gn_g, mem_norm_g,
           w_mem_kv, w_o, norm2_g, ffn_w_in, ffn_conv_w, ffn_conv_b, ffn_w_out, final_norm_g):
    bp, s_len, d = x_prompt.shape
    db, t_len, _ = x_sample.shape
    depth = norm1_g.shape[0]
    n_phys = cache_nsa_kv.shape[1]
    n_mem = mem_prompt.shape[1]
    fdim = ffn_w_out.shape[1]
    ret_heads = state_ret.shape[2]
    tok_w = ret_heads * RET_HEAD_DIM
    past = page_table.shape[1] * PAGE
    wb = state_nsa_win.shape[2]
    keep_p = min(WINDOW, s_len)
    assert cache_nsa_kv.shape[2] == PAGE and tok_w == NSA_Q_W and wb == WINDOW
    assert s_len % RET_CHUNK == 0 and t_len <= RET_CHUNK and t_len % 8 == 0

    xp = x_prompt.reshape(bp * s_len, d)
    xs = x_sample.reshape(db * t_len, d)
    cache = cache_nsa_kv.reshape(-1, KV_PLANES, HEAD_DIM)
    win_state = state_nsa_win.reshape(-1, wb, 2 * NSA_KVW)
    mem2 = mem_prompt.reshape(bp * n_mem, d)
    prompt_pages = jnp.arange(bp * s_len // PAGE, dtype=jnp.int32)
    pos_p = jnp.arange(s_len, dtype=jnp.int32)
    pos_s = past + jnp.minimum(jnp.arange(RET_CHUNK, dtype=jnp.int32), t_len - 1)
    ret_tab_p = _retention_tables(ret_heads, RET_CHUNK, RET_CHUNK, pos_p)
    ret_tab_s = _retention_tables(ret_heads, RET_CHUNK, t_len, pos_s)
    zero_state = jnp.zeros((bp, ret_heads, RET_HEAD_DIM, RET_HEAD_DIM), F32)
    zero_conv = jnp.zeros((bp, CONV_W - 1, fdim), F32)

    kv_p, kv_s, win_p, win_s, ret_p, ret_s, conv_p, conv_s, mem_p = ([] for _ in range(9))
    for i in range(depth):
        mem_kv_p = _norm_matmul(mem2, mem_norm_g[i], w_mem_kv[i].astype(BF16), n_mem, PROJ_TN)
        mem_kv_p = mem_kv_p.reshape(bp, n_mem, 2 * MEM_W)
        mem_p.append(mem_kv_p.reshape(bp, n_mem, 2, MEM_HEADS, HEAD_DIM))
        mem_kv_s = cache_mem_kv[i].reshape(db, n_mem, 2 * MEM_W)
        if i % 2 == 0:
            a = i // 2
            w_in = _prep_nsa_w(nsa_w_in[a])
            pe, w1, w2 = nsa_cmp_pe[a], nsa_cmp_w1[a], nsa_cmp_w2[a]
            proj_p = _norm_matmul(xp, norm1_g[i], w_in, PROJ_TM, PROJ_TN)
            proj_s = _norm_matmul(xs, norm1_g[i], w_in, db * t_len, PROJ_TN)
            p3 = proj_p.reshape(bp, s_len, NSA_PROJ_W)
            s3 = proj_s.reshape(db, t_len, NSA_PROJ_W)
            rows_lo, rows_hi = NSA_Q_W, NSA_Q_W + NSA_ROWS_W
            kv_p.append(p3[:, :, rows_lo:rows_hi].reshape(bp, s_len, 4, NSA_KV, HEAD_DIM))
            kv_s.append(s3[:, :, rows_lo:rows_hi].reshape(db, t_len, 4, NSA_KV, HEAD_DIM))
            cmp = _even_odd(_compress(kv_p[-1].reshape(-1, KV_PLANES, HEAD_DIM), prompt_pages, pe, w1, w2,
                                      CMP_PAGES), bp)
            tok_p = _nsa_prompt(p3, cmp[0], cmp[1], NSA_COLS, NSA_TQ, NSA_TK)
            page_ids = page_table + a * n_phys
            cmp = _even_odd(_compress(cache, page_ids.reshape(-1), pe, w1, w2, CMP_PAGES), db)
            tok_s = _nsa_sample(s3, cache, page_ids, cmp[0], cmp[1], win_state, a * db, past, SAMPLE_PAGES)
            qm_block = NSA_MAIN_W // MEM_W
            win_p.append(p3[:, s_len - keep_p:, rows_hi:NSA_MAIN_W].reshape(bp, keep_p, 2, NSA_KV, HEAD_DIM))
            new_win = s3[:, :, rows_hi:NSA_MAIN_W].reshape(db, t_len, 2, NSA_KV, HEAD_DIM)
            win_s.append(jnp.concatenate([state_nsa_win[a], new_win], axis=1)[:, -wb:])
        else:
            bl = i // 2
            w_in = ret_w_in[bl].astype(BF16)
            proj_p = _norm_matmul(xp, norm1_g[i], w_in, PROJ_TM, PROJ_TN)
            proj_s = _norm_matmul(xs, norm1_g[i], w_in, db * t_len, PROJ_TN)
            p3 = proj_p.reshape(bp, s_len, -1)
            s3 = proj_s.reshape(db, t_len, -1)
            tok_p, sp = _retention(p3, ret_gn_g[bl], zero_state, *ret_tab_p)
            s3_pad = jnp.pad(s3, ((0, 0), (0, RET_CHUNK - t_len), (0, 0)))
            tok_s, ss = _retention(s3_pad, ret_gn_g[bl], state_ret[bl], *ret_tab_s)
            tok_s = tok_s[:, :t_len]
            ret_p.append(sp)
            ret_s.append(ss)
            qm_block = 4 * tok_w // MEM_W
        mem_out_p = _mem_attend(p3, qm_block, mem_kv_p, MEM_TQ)
        mem_out_s = _mem_attend(s3, qm_block, mem_kv_s, t_len)
        wo = w_o[i].astype(BF16)
        xp = _wo(xp, tok_p.reshape(bp * s_len, tok_w), mem_out_p.reshape(bp * s_len, MEM_W), wo[:tok_w], wo[tok_w:],
                 WO_TM, WO_TN)
        xs = _wo(xs, tok_s.reshape(db * t_len, tok_w), mem_out_s.reshape(db * t_len, MEM_W), wo[:tok_w], wo[tok_w:],
                 db * t_len, WO_TN)
        last = i == depth - 1
        ffn_w = (norm2_g[i], ffn_w_in[i].astype(BF16), ffn_conv_w[i], ffn_conv_b[i], ffn_w_out[i].astype(BF16))
        xp, tails = _ffn(xp, *ffn_w, zero_conv, final_norm_g, seq_len=s_len, tm=FFN_TM, tf=FFN_TF, final_norm=last)
        conv_p.append(tails[s_len // FFN_TM - 1::s_len // FFN_TM])
        xs, tails = _ffn(xs, *ffn_w, state_ffn_conv[i], final_norm_g, seq_len=t_len, tm=t_len, tf=FFN_TF,
                         final_norm=last)
        conv_s.append(tails)
    return (xp.reshape(bp, s_len, d), xs.reshape(db, t_len, d), jnp.stack(kv_p), jnp.stack(kv_s), jnp.stack(win_p),
            jnp.stack(win_s), jnp.stack(ret_p), jnp.stack(ret_s), jnp.stack(conv_p), jnp.stack(conv_s),
            jnp.stack(mem_p))
```

```python
import functools

import jax
import jax.numpy as jnp
import numpy as np
from jax import lax
from jax.experimental import pallas as pl
from jax.experimental.pallas import tpu as pltpu

F32 = jnp.float32
BF16 = jnp.bfloat16

HEAD_DIM = 128
MEM_HEADS = 4
NSA_KV = 4
CMP_BLOCK = 32
SLC_BLOCK = 64
TOP_N = 16
WINDOW = 512
FORCE_BONUS = 1.0e4
RET_HEAD_DIM = 256
RET_CHUNK = 128
ROPE_BASE = 10000.0
CONV_W = 3
NORM_EPS = 1e-6
PAGE = 128

V7X_LANES = 128
V7X_VMEM_BYTES = 64 * 1024 * 1024
NEG_BIG = -1e30


def _cparams(n_axes, vmem_bytes):
    limit = min(int(vmem_bytes * 1.25) + (8 << 20), V7X_VMEM_BYTES - (4 << 20))
    return pltpu.CompilerParams(dimension_semantics=("arbitrary",) * n_axes, vmem_limit_bytes=limit)


def _dot(a, b):
    return jnp.dot(a, b, preferred_element_type=F32)


def _dot_nt(a, b):
    return lax.dot_general(a, b, (((1,), (1,)), ((), ())), preferred_element_type=F32)


def _rms(x, g):
    ms = jnp.mean(x * x, axis=-1, keepdims=True)
    return x * lax.rsqrt(ms + NORM_EPS) * g


def _norm_matmul_kernel(x_ref, g_ref, w_ref, o_ref, hn_ref):
    @pl.when(pl.program_id(1) == 0)
    def _():
        hn_ref[...] = _rms(x_ref[...], g_ref[...]).astype(BF16)

    o_ref[...] = _dot(hn_ref[...], w_ref[...])


def _norm_matmul(x, g, w, tm, tn):
    m, k = x.shape
    n = w.shape[1]
    vmem = 2 * tm * k * 4 + tm * k * 2 + 2 * k * tn * 2 + 2 * tm * tn * 4
    return pl.pallas_call(
        _norm_matmul_kernel,
        grid=(m // tm, n // tn),
        in_specs=[
            pl.BlockSpec((tm, k), lambda i, j: (i, 0)),
            pl.BlockSpec((1, k), lambda i, j: (0, 0)),
            pl.BlockSpec((k, tn), lambda i, j: (0, j)),
        ],
        out_specs=pl.BlockSpec((tm, tn), lambda i, j: (i, j)),
        out_shape=jax.ShapeDtypeStruct((m, n), F32),
        scratch_shapes=[pltpu.VMEM((tm, k), BF16)],
        compiler_params=_cparams(2, vmem),
        name="norm_matmul",
    )(x, g.reshape(1, k), w)


def _wo_kernel(x_ref, tok_ref, mem_ref, wt_ref, wm_ref, o_ref):
    o_ref[...] = (x_ref[...] + _dot(tok_ref[...].astype(BF16), wt_ref[...])
                  + _dot(mem_ref[...].astype(BF16), wm_ref[...]))


def _wo(x, tok, mem, w_tok, w_mem, tm, tn):
    m, d = x.shape
    kt, km = tok.shape[1], mem.shape[1]
    vmem = 2 * (tm * tn * 8 + tm * (kt + km) * 4 + (kt + km) * tn * 2)
    return pl.pallas_call(
        _wo_kernel,
        grid=(m // tm, d // tn),
        in_specs=[
            pl.BlockSpec((tm, tn), lambda i, j: (i, j)),
            pl.BlockSpec((tm, kt), lambda i, j: (i, 0)),
            pl.BlockSpec((tm, km), lambda i, j: (i, 0)),
            pl.BlockSpec((kt, tn), lambda i, j: (0, j)),
            pl.BlockSpec((km, tn), lambda i, j: (0, j)),
        ],
        out_specs=pl.BlockSpec((tm, tn), lambda i, j: (i, j)),
        out_shape=jax.ShapeDtypeStruct((m, d), F32),
        compiler_params=_cparams(2, vmem),
        name="wo_residual",
    )(x, tok, mem, w_tok, w_mem)


def _ffn_kernel(x_ref, g_ref, wa_ref, wg_ref, cw_ref, cb_ref, wo_ref, fg_ref, *refs,
                tiles_per_seq, seq_len, n_f, tn, final_norm):
    multi_seq = tiles_per_seq == 0
    prev_refs, (y_ref, tail_ref, hn_ref, act_ref, carry_ref) = refs[:-5], refs[-5:]
    i = pl.program_id(0)
    s = pl.program_id(1)
    tm, d = x_ref.shape

    @pl.when(s == 0)
    def _():
        hn_ref[...] = _rms(x_ref[...], g_ref[...]).astype(BF16)

    @pl.when(s < n_f)
    def _():
        hn = hn_ref[...]
        a = _dot(hn, wa_ref[...])
        gv = _dot(hn, wg_ref[...])
        row = lax.broadcasted_iota(jnp.int32, a.shape, 0)
        if multi_seq:
            t = row % seq_len
            a1 = jnp.where(t == 0, prev_refs[0][...], pltpu.roll(a, 1, axis=0))
            a2 = jnp.where(t < 2, prev_refs[1][...], pltpu.roll(a, 2, axis=0))
            tail_ref[...] = a
        else:
            prev = jnp.where((i % tiles_per_seq) == 0, prev_refs[0][0], carry_ref[s])
            a1 = jnp.where(row == 0, prev[1:2], pltpu.roll(a, 1, axis=0))
            a2 = jnp.where(row == 0, prev[0:1], jnp.where(row == 1, prev[1:2], pltpu.roll(a, 2, axis=0)))
            carry_ref[s] = a[tm - 2:tm]
            tail_ref[0] = a[tm - 2:tm]
        cw = cw_ref[...]
        ac = cb_ref[...] + a2 * cw[0:1] + a1 * cw[1:2] + a * cw[2:3]
        act_ref[s] = (ac * jax.nn.sigmoid(ac) * gv).astype(BF16)

    for n in range(d // tn):
        @pl.when(s == n_f + n)
        def _(n=n):
            act = jnp.concatenate([act_ref[k] for k in range(n_f)], axis=1)
            cols = slice(n * tn, (n + 1) * tn)
            y_ref[:, cols] = x_ref[:, cols] + _dot(act, wo_ref[...])

    if final_norm:
        @pl.when(s == n_f + d // tn - 1)
        def _():
            y_ref[...] = _rms(y_ref[...], fg_ref[...])


def _ffn(x, norm_g, w_in, conv_w, conv_b, w_out, buf, final_g, *, seq_len, tm, tf, tn, final_norm):
    m, d = x.shape
    fdim = w_out.shape[0]
    nf, nn = fdim // tf, d // tn
    multi_seq = tm > seq_len
    tiles_per_seq = 0 if multi_seq else seq_len // tm
    fa = lambda s: jnp.minimum(s, nf - 1)
    fb = lambda s: jnp.maximum(s - nf, 0)
    if multi_seq:
        n_seq = m // seq_len
        first = jnp.zeros((n_seq, seq_len, fdim), F32).at[:, 0].set(buf[:, 1])
        both = first.at[:, 0].set(buf[:, 0]).at[:, 1].set(buf[:, 1])
        prev = [first.reshape(m, fdim), both.reshape(m, fdim)]
        prev_specs = [pl.BlockSpec((tm, tf), lambda i, s: (i, fa(s)))] * 2
        tail_spec = pl.BlockSpec((tm, tf), lambda i, s: (i, fa(s)))
        tail_shape = jax.ShapeDtypeStruct((m, fdim), F32)
    else:
        prev = [buf]
        prev_specs = [pl.BlockSpec((1, CONV_W - 1, tf), lambda i, s: (i // tiles_per_seq, 0, fa(s)))]
        tail_spec = pl.BlockSpec((1, CONV_W - 1, tf), lambda i, s: (i, 0, fa(s)))
        tail_shape = jax.ShapeDtypeStruct((m // tm, CONV_W - 1, fdim), F32)
    vmem = (2 * tm * d * 4 + tm * d * 2 + 2 * tm * fdim * 2 + 2 * (2 * d * tf * 2 + fdim * tn * 2)
            + 10 * tm * tf * 4 + nf * 8 * tf * 4)
    kern = functools.partial(_ffn_kernel, tiles_per_seq=tiles_per_seq, seq_len=seq_len, n_f=nf, tn=tn,
                             final_norm=final_norm)
    return pl.pallas_call(
        kern,
        grid=(m // tm, nf + nn),
        in_specs=[
            pl.BlockSpec((tm, d), lambda i, s: (i, 0), pipeline_mode=pl.Buffered(1)),
            pl.BlockSpec((1, d), lambda i, s: (0, 0)),
            pl.BlockSpec((d, tf), lambda i, s: (0, fa(s))),
            pl.BlockSpec((d, tf), lambda i, s: (0, fa(s) + nf)),
            pl.BlockSpec((CONV_W, tf), lambda i, s: (0, fa(s))),
            pl.BlockSpec((1, tf), lambda i, s: (0, fa(s))),
            pl.BlockSpec((fdim, tn), lambda i, s: (0, fb(s))),
            pl.BlockSpec((1, d), lambda i, s: (0, 0)),
        ] + prev_specs,
        out_specs=[pl.BlockSpec((tm, d), lambda i, s: (i, 0), pipeline_mode=pl.Buffered(1)), tail_spec],
        out_shape=[jax.ShapeDtypeStruct((m, d), F32), tail_shape],
        scratch_shapes=[pltpu.VMEM((tm, d), BF16), pltpu.VMEM((nf, tm, tf), BF16),
                        pltpu.VMEM((nf, CONV_W - 1, tf), F32)],
        compiler_params=_cparams(2, vmem),
        name="conv_ffn",
    )(x, norm_g.reshape(1, d), w_in, w_in, conv_w, conv_b.reshape(1, fdim), w_out, final_g.reshape(1, d), *prev)


def _mem_kernel(q_ref, kv_ref, o_ref):
    q = q_ref[0] * (HEAD_DIM ** -0.5)
    kv = kv_ref[0]
    width = MEM_HEADS * HEAD_DIM
    outs = []
    for h in range(MEM_HEADS):
        lo = h * HEAD_DIM
        qh = q[:, lo:lo + HEAD_DIM].astype(BF16)
        kh = kv[:, lo:lo + HEAD_DIM].astype(BF16)
        vh = kv[:, width + lo:width + lo + HEAD_DIM].astype(BF16)
        s = _dot_nt(qh, kh)
        e = jnp.exp(s - jnp.max(s, axis=-1, keepdims=True))
        p = e / jnp.sum(e, axis=-1, keepdims=True)
        outs.append(_dot(p.astype(BF16), vh))
    o_ref[0] = jnp.concatenate(outs, axis=-1)


def _mem_attend(proj, qm_col_block, mem_kv, tq):
    b, t, _ = proj.shape
    width = MEM_HEADS * HEAD_DIM
    n_mem = mem_kv.shape[1]
    vmem = 2 * (2 * tq * width * 4 + n_mem * 2 * width * 4) + 8 * tq * n_mem * 4
    return pl.pallas_call(
        _mem_kernel,
        grid=(b, t // tq),
        in_specs=[
            pl.BlockSpec((1, tq, width), lambda bi, i: (bi, i, qm_col_block)),
            pl.BlockSpec((1, n_mem, 2 * width), lambda bi, i: (bi, 0, 0)),
        ],
        out_specs=pl.BlockSpec((1, tq, width), lambda bi, i: (bi, i, 0)),
        out_shape=jax.ShapeDtypeStruct((b, t, width), F32),
        compiler_params=_cparams(2, vmem),
        name="mem_attend",
    )(proj, mem_kv)


def _ret_kernel(q_ref, k_ref, v_ref, gate_ref, cos_ref, sin_ref, dmat_ref, qd_ref, kd_ref, cd_ref,
                gn_ref, s0_ref, tok_ref, s_out_ref, s_ref, *, n_chunks):
    c = pl.program_id(2)

    @pl.when(c == 0)
    def _():
        s_ref[...] = s0_ref[0, 0]

    half = RET_HEAD_DIM // 2
    cos = cos_ref[...]
    sin = sin_ref[...]

    def rot(x):
        x1, x2 = x[:, :half], x[:, half:]
        return jnp.concatenate([x1 * cos - x2 * sin, x1 * sin + x2 * cos], axis=-1)

    q = rot(q_ref[0])
    k = rot(k_ref[0]) * (RET_HEAD_DIM ** -0.5)
    v = v_ref[0].astype(BF16)
    qb = q.astype(BF16)
    s_old = s_ref[...]
    inner = _dot_nt(qb, k.astype(BF16)) * dmat_ref[0]
    o = _dot(inner.astype(BF16), v) + _dot(qb, s_old.astype(BF16)) * qd_ref[0]
    kd = (k * kd_ref[0]).astype(BF16)
    s_new = s_old * cd_ref[0] + _dot(kd.T, v)
    s_ref[...] = s_new

    mu = jnp.mean(o, axis=-1, keepdims=True)
    dev = o - mu
    var = jnp.mean(dev * dev, axis=-1, keepdims=True)
    y = dev * lax.rsqrt(var + NORM_EPS) * gn_ref[0]
    gate = gate_ref[0]
    tok_ref[0] = gate * jax.nn.sigmoid(gate) * y

    @pl.when(c == n_chunks - 1)
    def _():
        s_out_ref[0, 0] = s_new


def _retention(proj, gn_g, s0, cos, sin, dmat, q_decay, k_decay, c_decay):
    b, t, _ = proj.shape
    h = s0.shape[1]
    dk = RET_HEAD_DIM
    c = dmat.shape[1]
    n = t // c
    vmem = 2 * (5 * c * dk * 4 + c * dk * 4 + c * c * 4 + 2 * c * 128 * 4 + 2 * dk * dk * 4) + 3 * dk * dk * 4
    blk = lambda off: pl.BlockSpec((1, c, dk), lambda bi, hi, ci: (bi, ci, off * h + hi))
    return pl.pallas_call(
        functools.partial(_ret_kernel, n_chunks=n),
        grid=(b, h, n),
        in_specs=[
            blk(0), blk(1), blk(2), blk(3),
            pl.BlockSpec((c, dk // 2), lambda bi, hi, ci: (ci, 0)),
            pl.BlockSpec((c, dk // 2), lambda bi, hi, ci: (ci, 0)),
            pl.BlockSpec((1, c, c), lambda bi, hi, ci: (hi, 0, 0)),
            pl.BlockSpec((1, c, 1), lambda bi, hi, ci: (hi, 0, 0)),
            pl.BlockSpec((1, c, 1), lambda bi, hi, ci: (hi, 0, 0)),
            pl.BlockSpec((1, 1, dk), lambda bi, hi, ci: (hi, 0, 0)),
            pl.BlockSpec((1, 1, dk), lambda bi, hi, ci: (hi, 0, 0)),
            pl.BlockSpec((1, 1, dk, dk), lambda bi, hi, ci: (bi, hi, 0, 0)),
        ],
        out_specs=[
            pl.BlockSpec((1, c, dk), lambda bi, hi, ci: (bi, ci, hi)),
            pl.BlockSpec((1, 1, dk, dk), lambda bi, hi, ci: (bi, hi, 0, 0)),
        ],
        out_shape=[jax.ShapeDtypeStruct((b, t, h * dk), F32),
                   jax.ShapeDtypeStruct((b, h, dk, dk), F32)],
        scratch_shapes=[pltpu.VMEM((dk, dk), F32)],
        compiler_params=_cparams(3, vmem),
        name="retention",
    )(proj, proj, proj, proj, cos, sin, dmat, q_decay, k_decay, c_decay, gn_g.reshape(h, 1, dk), s0)


def _retention_tables(n_heads, c_pad, c_real, pos):
    log_g = jnp.log1p(-jnp.exp2(-5.0 - jnp.arange(n_heads, dtype=F32)))
    i = jnp.arange(c_pad, dtype=F32)
    live = i < c_real
    diff = i[:, None] - i[None, :]
    dmat = jnp.where((diff >= 0) & live[:, None] & live[None, :],
                     jnp.exp(jnp.maximum(diff, 0.0)[None] * log_g[:, None, None]), 0.0)
    q_decay = jnp.exp((i + 1.0)[None, :] * log_g[:, None])[..., None]
    k_decay = jnp.where(live[None, :], jnp.exp((c_real - 1.0 - i)[None, :] * log_g[:, None]), 0.0)[..., None]
    c_decay = jnp.broadcast_to(jnp.exp(c_real * log_g)[:, None, None], (n_heads, 1, RET_HEAD_DIM))
    half = RET_HEAD_DIM // 2
    inv = ROPE_BASE ** (-jnp.arange(half, dtype=F32) / half)
    ang = pos.astype(F32)[:, None] * inv[None, :]
    return jnp.cos(ang), jnp.sin(ang), dmat, q_decay, k_decay, c_decay


KV_PLANES = 4 * NSA_KV
CMP_PLANES = 2 * NSA_KV


def _compress_kernel(plist_ref, src_ref, pe_ref, w1_ref, w2_ref, o_ref, buf_ref, lhs_ref, sem_ref, *,
                     pages, n_steps, col0):
    s = pl.program_id(0)
    slot = s % 2
    bpp = PAGE // CMP_BLOCK
    nblk = pages * bpp
    rows = nblk * CMP_PLANES

    def copies(step, to_slot):
        cps = []
        for p in range(pages):
            tok0 = plist_ref[step * pages + p] * PAGE
            for n in range(bpp):
                toks = pl.ds(tok0 + n * CMP_BLOCK, CMP_BLOCK)
                if col0 is None:
                    src = src_ref.at[toks, pl.ds(0, CMP_PLANES), :]
                else:
                    src = src_ref.at[toks, pl.ds(col0, CMP_PLANES * HEAD_DIM)]
                cps.append(pltpu.make_async_copy(src, buf_ref.at[to_slot, :, p * bpp + n], sem_ref.at[to_slot]))
        return cps

    @pl.when(s == 0)
    def _():
        for cp in copies(0, 0):
            cp.start()

    @pl.when(s + 1 < n_steps)
    def _():
        for cp in copies(s + 1, 1 - slot):
            cp.start()

    for cp in copies(s, slot):
        cp.wait()

    row = lax.broadcasted_iota(jnp.int32, (rows, HEAD_DIM), 0)
    if col0 is None:
        for c in range(CMP_BLOCK):
            x = buf_ref[slot, c] + pe_ref[c]
            lhs_ref[:, c * HEAD_DIM:(c + 1) * HEAD_DIM] = x.reshape(rows, HEAD_DIM).astype(BF16)
        is_key = (row % CMP_PLANES) < NSA_KV
    else:
        for c in range(CMP_BLOCK):
            x = buf_ref[slot, c]
            for j in range(CMP_PLANES):
                xj = x[:, j * HEAD_DIM:(j + 1) * HEAD_DIM] + pe_ref[c, j:j + 1, :]
                lhs_ref[j * nblk:(j + 1) * nblk, c * HEAD_DIM:(c + 1) * HEAD_DIM] = xj.astype(BF16)
        is_key = row < NSA_KV * nblk
    h = _dot(lhs_ref[...], w1_ref[...])
    h = jax.nn.gelu(jnp.where(is_key, h[:, :HEAD_DIM], h[:, HEAD_DIM:]))
    out = _dot(h.astype(BF16), w2_ref[...])
    o_ref[...] = jnp.where(is_key, out[:, :HEAD_DIM], out[:, HEAD_DIM:]).reshape(o_ref.shape)


def _compress(src, plist, pe, w1, w2, pages, col0=None):
    n_pages = plist.shape[0]
    nblk = pages * (PAGE // CMP_BLOCK)
    n_total = n_pages * (PAGE // CMP_BLOCK)
    rows = nblk * CMP_PLANES
    k_dim = CMP_BLOCK * HEAD_DIM
    pe8 = jnp.repeat(pe.transpose(1, 0, 2), NSA_KV, axis=1)
    w1b = jnp.concatenate([w1[0], w1[1]], axis=1).astype(BF16)
    w2b = jnp.concatenate([w2[0], w2[1]], axis=1).astype(BF16)
    if col0 is None:
        buf_shape = (2, CMP_BLOCK, nblk, CMP_PLANES, HEAD_DIM)
        out_spec = pl.BlockSpec((nblk, CMP_PLANES, HEAD_DIM), lambda s, pr: (s, 0, 0))
        out_shape = (n_total, CMP_PLANES, HEAD_DIM)
    else:
        buf_shape = (2, CMP_BLOCK, nblk, CMP_PLANES * HEAD_DIM)
        out_spec = pl.BlockSpec((CMP_PLANES, nblk, HEAD_DIM), lambda s, pr: (0, s, 0))
        out_shape = (CMP_PLANES, n_total, HEAD_DIM)
    vmem = 2 * CMP_BLOCK * rows * HEAD_DIM * 4 + rows * k_dim * 2 + 2 * k_dim * 2 * HEAD_DIM * 2 + 8 * rows * HEAD_DIM * 4
    grid_spec = pltpu.PrefetchScalarGridSpec(
        num_scalar_prefetch=1,
        grid=(n_pages // pages,),
        in_specs=[
            pl.BlockSpec(memory_space=pl.ANY),
            pl.BlockSpec((CMP_BLOCK, CMP_PLANES, HEAD_DIM), lambda s, pr: (0, 0, 0)),
            pl.BlockSpec((k_dim, 2 * HEAD_DIM), lambda s, pr: (0, 0)),
            pl.BlockSpec((HEAD_DIM, 2 * HEAD_DIM), lambda s, pr: (0, 0)),
        ],
        out_specs=out_spec,
        scratch_shapes=[pltpu.VMEM(buf_shape, F32), pltpu.VMEM((rows, k_dim), BF16), pltpu.SemaphoreType.DMA((2,))],
    )
    return pl.pallas_call(
        functools.partial(_compress_kernel, pages=pages, n_steps=n_pages // pages, col0=col0),
        grid_spec=grid_spec,
        out_shape=jax.ShapeDtypeStruct(out_shape, F32),
        compiler_params=_cparams(1, vmem),
        name="nsa_compress",
    )(plist, src, pe8, w1b, w2b)


def _even_odd(c, n_seq, planes_first=False):
    d = c.shape[-1]
    nc = c.shape[1 if planes_first else 0] // n_seq
    if planes_first:
        c = c.reshape(2, NSA_KV, n_seq, nc // 2, 2, d).transpose(0, 1, 2, 4, 3, 5)
    else:
        c = c.reshape(n_seq, nc // 2, 2, 2, NSA_KV, d).transpose(3, 4, 0, 2, 1, 5)
    return c.reshape(2, NSA_KV, n_seq, nc, d).astype(BF16)


def _cmp_attend(qb, kc, vc, qpos_col):
    nc = kc.shape[0]
    lane = lax.broadcasted_iota(jnp.int32, (1, nc), 1)
    blk = jnp.where(lane < nc // 2, 2 * lane, 2 * (lane - nc // 2) + 1)
    vis = (blk * CMP_BLOCK + (CMP_BLOCK - 1)) <= qpos_col
    s = jnp.where(vis, _dot_nt(qb, kc), -jnp.inf)
    m = jnp.max(s, axis=-1, keepdims=True)
    m = jnp.where(m > -jnp.inf, m, 0.0)
    e = jnp.where(vis, jnp.exp(s - m), 0.0)
    p = e / jnp.maximum(jnp.sum(e, axis=-1, keepdims=True), 1e-30)
    return _dot(p.astype(BF16), vc), p


def _select_blocks(p_t, qpos_row, n_sel, m_hi, score_ref):
    nb, nq = p_t.shape
    blk = lax.broadcasted_iota(jnp.int32, (nb, nq), 0)
    valid = blk * SLC_BLOCK <= qpos_row
    cur = qpos_row // SLC_BLOCK
    forced = (blk == 0) | (blk == cur) | (blk == cur - 1)
    score = jnp.where(valid, p_t + jnp.where(forced, FORCE_BONUS, 0.0), -jnp.inf)
    score_ref[...] = score
    rows = min(64, nb)
    outs = []
    for r0 in range(0, nb, rows):
        sc = score[r0:r0 + rows]
        bk = blk[r0:r0 + rows]

        def body(mi, rank, sc=sc, bk=bk):
            other = score_ref[pl.ds(mi, 1), :]
            ge = jnp.where(other >= sc, 1.0, 0.0)
            gt = jnp.where(other > sc, 1.0, 0.0)
            return rank + jnp.where(bk > mi, ge, gt)

        rank = lax.fori_loop(0, m_hi, body, jnp.zeros_like(sc))
        outs.append(jnp.where((rank < n_sel) & valid[r0:r0 + rows], 1.0, 0.0))
    return jnp.concatenate(outs, axis=0) if len(outs) > 1 else outs[0]


def _lane_tile(x, n_lanes):
    reps = n_lanes // x.shape[1]
    return x if reps == 1 else jnp.concatenate([x] * reps, axis=1)


def _softmax_step(s, vis, v, m_ref, l_ref, acc_ref, r0, value_shift=0):
    rows, n_keys = s.shape
    sl = pl.ds(r0, rows)
    s = jnp.where(vis, s, NEG_BIG)
    m_old = m_ref[sl, :]
    m_new = jnp.maximum(m_old, jnp.max(s, axis=-1, keepdims=True))
    alpha = jnp.exp(m_old - m_new)
    p = jnp.exp(s - _lane_tile(m_new, n_keys))
    l_ref[sl, :] = alpha * l_ref[sl, :] + jnp.sum(p, axis=-1, keepdims=True)
    pv = pltpu.roll(p, value_shift, axis=1) if value_shift else p
    acc_ref[sl, :] = alpha * acc_ref[sl, :] + _dot(pv.astype(BF16), v)
    m_ref[sl, :] = m_new


def _softmax_finish(m_ref, l_ref, acc_ref):
    seen = m_ref[...] > 0.5 * NEG_BIG
    return jnp.where(seen, acc_ref[...] / jnp.maximum(l_ref[...], 1e-30), 0.0)


def _softmax_once(s, vis, v):
    groups = s.shape[1] // V7X_LANES
    s = jnp.where(vis, s, NEG_BIG)
    part = lambda x, c: x[:, c * V7X_LANES:(c + 1) * V7X_LANES]
    m = functools.reduce(jnp.maximum, [part(s, c) for c in range(groups)])
    m = jnp.max(m, axis=-1, keepdims=True)
    e = jnp.exp(s - m)
    l = functools.reduce(jnp.add, [part(e, c) for c in range(groups)])
    l = jnp.maximum(jnp.sum(l, axis=-1, keepdims=True), 1e-30)
    return jnp.where(m > 0.5 * NEG_BIG, _dot(e.astype(BF16), v) / l, 0.0)


def _block_expander(n_blocks, key):
    blk = lax.broadcasted_iota(jnp.int32, (n_blocks, key.shape[1]), 0)
    return jnp.where(blk == key // SLC_BLOCK, 1.0, 0.0).astype(BF16)


def _nsa_prompt_kernel(q_ref, kc_ref, vc_ref, ks_ref, vs_ref, kw_ref, vw_ref, gate_ref, tok_ref,
                       score_ref, s_ref, m_ref, l_ref, acc_ref, *, tk):
    i = pl.program_id(2)
    tq = q_ref.shape[1]
    r = q_ref.shape[2] // HEAD_DIM
    nb = kc_ref.shape[2] // (SLC_BLOCK // CMP_BLOCK)
    nbp = -(-nb // V7X_LANES) * V7X_LANES
    t0 = i * tq
    q = q_ref[0] * (HEAD_DIM ** -0.5)
    qb = jnp.concatenate([q[:, h * HEAD_DIM:(h + 1) * HEAD_DIM] for h in range(r)], axis=0).astype(BF16)
    tcol = t0 + lax.broadcasted_iota(jnp.int32, (tq, 1), 0)
    trow = t0 + lax.broadcasted_iota(jnp.int32, (1, tq), 1)

    o_cmp, p = _cmp_attend(qb, kc_ref[0, 0], vc_ref[0, 0], jnp.concatenate([tcol] * r, axis=0))
    p_grp = p[0:tq]
    for h in range(1, r):
        p_grp = p_grp + p[h * tq:(h + 1) * tq]
    p_pair = p_grp + pltpu.roll(p_grp, nb, axis=1)
    m_hi = jnp.minimum((t0 + tq - 1) // SLC_BLOCK + 1, nb)
    sel_t = _select_blocks(p_pair.T[:nb], trow, TOP_N, m_hi, score_ref)
    if nbp > nb:
        sel_t = jnp.concatenate([sel_t, jnp.zeros((nbp - nb, tq), F32)], axis=0)
    sel = sel_t.T.astype(BF16)

    n_kt = (t0 + tq + tk - 1) // tk
    lane_groups = tk // V7X_LANES
    m_ref[...] = jnp.full(m_ref.shape, NEG_BIG, F32)

    def score_tile(j, carry):
        k0 = pl.multiple_of(j * tk, tk)
        s = _dot_nt(qb, ks_ref[0, pl.ds(k0, tk), :].astype(BF16))
        picked = _dot(sel, _block_expander(nbp, k0 + lax.broadcasted_iota(jnp.int32, (1, tk), 1)))
        kpos = k0 + lax.broadcasted_iota(jnp.int32, (tq, tk), 1)
        vis = (picked > 0.5) & (kpos <= tcol)
        for h in range(r):
            rows = pl.ds(h * tq, tq)
            sh = jnp.where(vis, s[h * tq:(h + 1) * tq], NEG_BIG)
            s_ref[j, rows, :] = sh
            m = m_ref[rows, :]
            for c in range(lane_groups):
                m = jnp.maximum(m, sh[:, c * V7X_LANES:(c + 1) * V7X_LANES])
            m_ref[rows, :] = m
        return carry

    lax.fori_loop(0, n_kt, score_tile, 0)
    m_row = jnp.max(m_ref[...], axis=-1, keepdims=True)
    m_ref[...] = jnp.broadcast_to(m_row, m_ref.shape)
    l_ref[...] = jnp.zeros(l_ref.shape, F32)
    acc_ref[...] = jnp.zeros(acc_ref.shape, F32)

    def value_tile(j, carry):
        k0 = pl.multiple_of(j * tk, tk)
        p = jnp.exp(s_ref[j] - _lane_tile(m_ref[...], tk))
        l = l_ref[...]
        for c in range(lane_groups):
            l = l + p[:, c * V7X_LANES:(c + 1) * V7X_LANES]
        l_ref[...] = l
        acc_ref[...] += _dot(p.astype(BF16), vs_ref[0, pl.ds(k0, tk), :].astype(BF16))
        return carry

    lax.fori_loop(0, n_kt, value_tile, 0)
    denom = jnp.maximum(jnp.sum(l_ref[...], axis=-1, keepdims=True), 1e-30)
    o_slc = jnp.where(m_row > 0.5 * NEG_BIG, acc_ref[...] / denom, 0.0)

    span = WINDOW + tq
    w0 = pl.multiple_of(jnp.maximum(t0 + tq - span, 0), tq)
    kw = kw_ref[0, pl.ds(w0, span), :].astype(BF16)
    vw = vw_ref[0, pl.ds(w0, span), :].astype(BF16)
    s = _dot_nt(qb, kw)
    kpos = w0 + lax.broadcasted_iota(jnp.int32, (tq, span), 1)
    vis = (kpos <= tcol) & (kpos > tcol - WINDOW)
    o_win = [_softmax_once(s[h * tq:(h + 1) * tq], vis, vw) for h in range(r)]

    gt = jax.nn.sigmoid(gate_ref[0])
    outs = []
    for h in range(r):
        rows = slice(h * tq, (h + 1) * tq)
        outs.append(gt[:, 3 * h:3 * h + 1] * o_cmp[rows] + gt[:, 3 * h + 1:3 * h + 2] * o_slc[rows]
                    + gt[:, 3 * h + 2:3 * h + 3] * o_win[h])
    tok_ref[0] = jnp.concatenate(outs, axis=-1)


def _nsa_prompt(proj, kc, vc, cols, tq, tk):
    b, t, _ = proj.shape
    g = kc.shape[0]
    nc = kc.shape[2]
    r = 3
    nb = nc // (SLC_BLOCK // CMP_BLOCK)
    assert t >= WINDOW + tq and t % tk == 0 and tk % tq == 0 and nb % 64 == 0 and nc <= V7X_LANES
    kv_spec = lambda c0: pl.BlockSpec((1, t, HEAD_DIM), lambda bi, gi, i: (bi, 0, c0 + gi))
    cmp_spec = pl.BlockSpec((1, 1, nc, HEAD_DIM), lambda bi, gi, i: (gi, bi, 0, 0))
    vmem = 2 * 4 * t * HEAD_DIM * 4 + r * tq * t * 4 + 24 * r * tq * max(tk, WINDOW + tq) * 4
    return pl.pallas_call(
        functools.partial(_nsa_prompt_kernel, tk=tk),
        grid=(b, g, t // tq),
        in_specs=[
            pl.BlockSpec((1, tq, r * HEAD_DIM), lambda bi, gi, i: (bi, i, cols["q"] // r + gi)),
            cmp_spec, cmp_spec,
            kv_spec(cols["ks"]), kv_spec(cols["vs"]), kv_spec(cols["kw"]), kv_spec(cols["vw"]),
            pl.BlockSpec((1, tq, HEAD_DIM), lambda bi, gi, i: (bi, i, cols["gates"] + gi)),
        ],
        out_specs=pl.BlockSpec((1, tq, r * HEAD_DIM), lambda bi, gi, i: (bi, i, gi)),
        out_shape=jax.ShapeDtypeStruct((b, t, g * r * HEAD_DIM), F32),
        scratch_shapes=[pltpu.VMEM((nb, tq), F32), pltpu.VMEM((t // tk, r * tq, tk), F32),
                        pltpu.VMEM((r * tq, V7X_LANES), F32), pltpu.VMEM((r * tq, V7X_LANES), F32),
                        pltpu.VMEM((r * tq, HEAD_DIM), F32)],
        compiler_params=_cparams(3, vmem),
        name="nsa_prompt",
    )(proj, kc, vc, proj, proj, proj, proj, proj)


NSA_HEADS = 12
NSA_KVW = NSA_KV * HEAD_DIM
NSA_Q_W = NSA_HEADS * HEAD_DIM
NSA_ROWS_W = 4 * NSA_KVW
NSA_WIN_W = 2 * NSA_KVW
MEM_W = MEM_HEADS * HEAD_DIM
NSA_GATES = 3 * NSA_HEADS
NSA_MAIN_W = NSA_Q_W + NSA_ROWS_W + NSA_WIN_W
NSA_PROJ_W = NSA_MAIN_W + MEM_W + NSA_KV * V7X_LANES
NSA_COLS = {
    "q": 0,
    "ks": (NSA_Q_W + 2 * NSA_KVW) // V7X_LANES,
    "vs": (NSA_Q_W + 3 * NSA_KVW) // V7X_LANES,
    "kw": (NSA_Q_W + NSA_ROWS_W) // V7X_LANES,
    "vw": (NSA_Q_W + NSA_ROWS_W + NSA_KVW) // V7X_LANES,
    "gates": (NSA_MAIN_W + MEM_W) // V7X_LANES,
}


def _prep_nsa_w(w):
    d = w.shape[0]
    gates = w[:, NSA_MAIN_W:NSA_MAIN_W + NSA_GATES].reshape(d, NSA_KV, NSA_GATES // NSA_KV)
    gates = jnp.pad(gates, ((0, 0), (0, 0), (0, V7X_LANES - NSA_GATES // NSA_KV))).reshape(d, NSA_KV * V7X_LANES)
    return jnp.concatenate([w[:, :NSA_MAIN_W], w[:, NSA_MAIN_W + NSA_GATES:], gates], axis=1).astype(BF16)


def _stack_heads(q, n_heads):
    return jnp.concatenate([q[:, h * HEAD_DIM:(h + 1) * HEAD_DIM] for h in range(n_heads)], axis=0)


def _nsa_sample_cmp_kernel(q_ref, kc_ref, vc_ref, ocmp_ref, pslc_ref, *, past):
    t = q_ref.shape[1]
    g_n = kc_ref.shape[0]
    r = q_ref.shape[2] // HEAD_DIM // g_n
    nb = kc_ref.shape[2] // (SLC_BLOCK // CMP_BLOCK)
    q = q_ref[0] * (HEAD_DIM ** -0.5)
    qpos = past + lax.broadcasted_iota(jnp.int32, (t, 1), 0)
    qpos_r = jnp.concatenate([qpos] * r, axis=0)
    for g in range(g_n):
        qb = _stack_heads(q[:, g * r * HEAD_DIM:(g + 1) * r * HEAD_DIM], r).astype(BF16)
        o, p = _cmp_attend(qb, kc_ref[g, 0], vc_ref[g, 0], qpos_r)
        ocmp_ref[0, g * r * t:(g + 1) * r * t, :] = o
        p_grp = p[0:t]
        for h in range(1, r):
            p_grp = p_grp + p[h * t:(h + 1) * t]
        p_pair = p_grp + pltpu.roll(p_grp, nb, axis=1)
        pslc_ref[0, g * t:(g + 1) * t, :] = p_pair[:, :nb]


def _nsa_sample_cmp(proj, kc, vc, past):
    b, t, _ = proj.shape
    g, _, nc, _ = kc.shape
    nb = nc // (SLC_BLOCK // CMP_BLOCK)
    cmp_spec = pl.BlockSpec((g, 1, nc, HEAD_DIM), lambda bi: (0, bi, 0, 0))
    vmem = 2 * (t * NSA_Q_W * 4 + 2 * g * nc * HEAD_DIM * 2) + 64 * nc * 4 * 8
    return pl.pallas_call(
        functools.partial(_nsa_sample_cmp_kernel, past=past),
        grid=(b,),
        in_specs=[pl.BlockSpec((1, t, NSA_Q_W), lambda bi: (bi, 0, 0)), cmp_spec, cmp_spec],
        out_specs=[pl.BlockSpec((1, NSA_HEADS * t, HEAD_DIM), lambda bi: (bi, 0, 0)),
                   pl.BlockSpec((1, g * t, nb), lambda bi: (bi, 0, 0))],
        out_shape=[jax.ShapeDtypeStruct((b, NSA_HEADS * t, HEAD_DIM), F32),
                   jax.ShapeDtypeStruct((b, g * t, nb), F32)],
        compiler_params=_cparams(1, vmem),
        name="nsa_sample_cmp",
    )(proj, kc, vc)


def _rank_kernel(p_ref, qpos_ref, sel_ref, score_ref, *, n_sel):
    sel_ref[...] = _select_blocks(p_ref[...], qpos_ref[...], n_sel, p_ref.shape[0], score_ref)


def _rank_blocks(p_t, qpos, n_sel):
    nb, nq = p_t.shape
    tq = V7X_LANES
    return pl.pallas_call(
        functools.partial(_rank_kernel, n_sel=n_sel),
        grid=(nq // tq,),
        in_specs=[pl.BlockSpec((nb, tq), lambda i: (0, i)), pl.BlockSpec((1, tq), lambda i: (0, i))],
        out_specs=pl.BlockSpec((nb, tq), lambda i: (0, i)),
        out_shape=jax.ShapeDtypeStruct((nb, nq), F32),
        scratch_shapes=[pltpu.VMEM((nb, tq), F32)],
        compiler_params=_cparams(1, 8 * nb * tq * 4),
        name="nsa_rank",
    )(p_t, qpos)


def _nsa_sample_attend_kernel(pages_ref, *refs, past, n_steps, pps):
    kv_refs = refs[:pps]
    (q_ref, sel_ref, ocmp_ref, ksn_ref, vsn_ref, kwn_ref, vwn_ref, wst_ref, gate_ref, tok_ref,
     qs_ref, m_ref, l_ref, acc_ref) = refs[pps:]
    j = pl.program_id(1)
    t = q_ref.shape[1]
    g_n = NSA_KV
    r = NSA_HEADS // g_n
    rg = r * t
    nb = sel_ref.shape[2]
    span = pps * PAGE

    @pl.when(j == 0)
    def _():
        qs_ref[...] = _stack_heads(q_ref[0] * (HEAD_DIM ** -0.5), NSA_HEADS)
        m_ref[...] = jnp.full(m_ref.shape, NEG_BIG, F32)
        l_ref[...] = jnp.zeros(l_ref.shape, F32)
        acc_ref[...] = jnp.zeros(acc_ref.shape, F32)

    half = KV_PLANES // 2
    cols = span * half
    kvb = jnp.concatenate([kv[...].reshape(PAGE * half, HEAD_DIM) for kv in kv_refs], axis=0).astype(BF16)
    s = _dot_nt(qs_ref[...].astype(BF16), kvb)
    lane = lax.broadcasted_iota(jnp.int32, (g_n * t, cols), 1)
    own = lax.broadcasted_iota(jnp.int32, (g_n * t, cols), 0) // t
    picked = _dot(sel_ref[0], _block_expander(nb, j * span + lane[0:1] // half))
    hit = jnp.where((picked > 0.5) & ((lane % half) == own), 1.0, 0.0)
    vis = jnp.concatenate([hit[g * t:(g + 1) * t] for g in range(g_n) for _ in range(r)], axis=0) > 0.5
    _softmax_step(s, vis, kvb, m_ref, l_ref, acc_ref, 0, value_shift=g_n)

    @pl.when(j == n_steps - 1)
    def _():
        tcol = jnp.concatenate([lax.broadcasted_iota(jnp.int32, (t, 1), 0)] * r, axis=0)
        pad = jnp.zeros((PAGE - t, HEAD_DIM), F32)

        def new_keys(ref, g):
            return jnp.concatenate([ref[0][:, g * HEAD_DIM:(g + 1) * HEAD_DIM], pad], axis=0)

        lane = lax.broadcasted_iota(jnp.int32, (rg, PAGE), 1)
        for g in range(g_n):
            qg = qs_ref[g * rg:(g + 1) * rg, :].astype(BF16)
            _softmax_step(_dot_nt(qg, new_keys(ksn_ref, g).astype(BF16)), lane <= tcol,
                          new_keys(vsn_ref, g).astype(BF16), m_ref, l_ref, acc_ref, g * rg)
        o_slc = _softmax_finish(m_ref, l_ref, acc_ref)

        wb = wst_ref.shape[1]
        kpos = past - wb + lax.broadcasted_iota(jnp.int32, (rg, wb + PAGE), 1)
        qpos = past + tcol
        vis = (kpos <= qpos) & (kpos > qpos - WINDOW)
        wst = wst_ref[0]
        gates = gate_ref[0]
        ocmp = ocmp_ref[0]
        outs = []
        for g in range(g_n):
            qg = qs_ref[g * rg:(g + 1) * rg, :].astype(BF16)
            kw = jnp.concatenate([wst[:, g * HEAD_DIM:(g + 1) * HEAD_DIM], new_keys(kwn_ref, g)], axis=0)
            vw = jnp.concatenate([wst[:, NSA_KVW + g * HEAD_DIM:NSA_KVW + (g + 1) * HEAD_DIM],
                                  new_keys(vwn_ref, g)], axis=0)
            o_win = _softmax_once(_dot_nt(qg, kw.astype(BF16)), vis, vw.astype(BF16))
            gt = jax.nn.sigmoid(gates[:, g * V7X_LANES:(g + 1) * V7X_LANES])
            for h in range(r):
                rows = slice(g * rg + h * t, g * rg + (h + 1) * t)
                outs.append(gt[:, 3 * h:3 * h + 1] * ocmp[rows] + gt[:, 3 * h + 1:3 * h + 2] * o_slc[rows]
                            + gt[:, 3 * h + 2:3 * h + 3] * o_win[h * t:(h + 1) * t])
        tok_ref[0] = jnp.concatenate(outs, axis=-1)


def _nsa_sample_attend(cache, pages, proj, sel, ocmp, win_state, win_index0, past, pps):
    b, t, _ = proj.shape
    npg = pages.shape[0] // b
    nb = sel.shape[2]
    wb = win_state.shape[1]
    n_steps = npg // pps
    assert nb * SLC_BLOCK == npg * PAGE == past and t <= SLC_BLOCK and npg % pps == 0
    col = lambda first: first * V7X_LANES // NSA_KVW
    new_spec = lambda name: pl.BlockSpec((1, t, NSA_KVW), lambda bi, j, pg: (bi, 0, col(NSA_COLS[name])))
    page_spec = lambda k: pl.BlockSpec((PAGE, KV_PLANES // 2, HEAD_DIM),
                                       lambda bi, j, pg: (pg[bi * npg + j * pps + k], 1, 0))
    rows = NSA_HEADS * t
    vmem = (2 * (pps * PAGE * NSA_KVW * 2 * 4 + wb * 2 * NSA_KVW * 4 + t * NSA_PROJ_W * 4)
            + 6 * pps * PAGE * HEAD_DIM * 4 + 8 * (wb + PAGE) * 128 * 4)
    grid_spec = pltpu.PrefetchScalarGridSpec(
        num_scalar_prefetch=1,
        grid=(b, n_steps),
        in_specs=[page_spec(k) for k in range(pps)] + [
            pl.BlockSpec((1, t, NSA_Q_W), lambda bi, j, pg: (bi, 0, 0)),
            pl.BlockSpec((1, NSA_KV * t, nb), lambda bi, j, pg: (bi, 0, 0)),
            pl.BlockSpec((1, rows, HEAD_DIM), lambda bi, j, pg: (bi, 0, 0)),
            new_spec("ks"), new_spec("vs"), new_spec("kw"), new_spec("vw"),
            pl.BlockSpec((1, wb, 2 * NSA_KVW), lambda bi, j, pg: (win_index0 + bi, 0, 0)),
            pl.BlockSpec((1, t, NSA_KV * V7X_LANES), lambda bi, j, pg: (bi, 0, col(NSA_COLS["gates"]))),
        ],
        out_specs=pl.BlockSpec((1, t, NSA_Q_W), lambda bi, j, pg: (bi, 0, 0)),
        scratch_shapes=[pltpu.VMEM((rows, HEAD_DIM), F32), pltpu.VMEM((rows, V7X_LANES), F32),
                        pltpu.VMEM((rows, V7X_LANES), F32), pltpu.VMEM((rows, HEAD_DIM), F32)],
    )
    return pl.pallas_call(
        functools.partial(_nsa_sample_attend_kernel, past=past, n_steps=n_steps, pps=pps),
        grid_spec=grid_spec,
        out_shape=jax.ShapeDtypeStruct((b, t, NSA_Q_W), F32),
        compiler_params=_cparams(2, vmem),
        name="nsa_sample_attend",
    )(pages, *([cache] * pps), proj, sel, ocmp, proj, proj, proj, proj, win_state, proj)


def _nsa_sample(proj, cache, page_ids, kc, vc, win_state, win_index0, past, pps):
    b, t, _ = proj.shape
    ocmp, pslc = _nsa_sample_cmp(proj, kc, vc, past)
    nb = pslc.shape[2]
    qpos = jnp.broadcast_to(past + jnp.arange(t, dtype=jnp.int32), (b * NSA_KV, t)).reshape(1, -1)
    sel_t = _rank_blocks(pslc.reshape(b * NSA_KV * t, nb).T, qpos, TOP_N - 1)
    sel = sel_t.T.reshape(b, NSA_KV * t, nb).astype(BF16)
    return _nsa_sample_attend(cache, page_ids.reshape(-1), proj, sel, ocmp, win_state, win_index0, past, pps)


PROJ_TM = 1024
PROJ_TN = 512
WO_TM = 512
FFN_TM = 512
FFN_TF = 512
FFN_TN = 512
WO_TN = 1024
MEM_TQ = 512
NSA_TQ = 128
NSA_TK = 512
CMP_PAGES = 16
SAMPLE_PAGES = 8


def kernel(x_prompt, x_sample, cache_nsa_kv, state_nsa_win, state_ret, state_ffn_conv, cache_mem_kv, page_table,
           mem_prompt, norm1_g, nsa_w_in, nsa_cmp_pe, nsa_cmp_w1, nsa_cmp_w2, ret_w_in, ret_gn_g, mem_norm_g,
           w_mem_kv, w_o, norm2_g, ffn_w_in, ffn_conv_w, ffn_conv_b, ffn_w_out, final_norm_g):
    bp, s_len, d = x_prompt.shape
    db, t_len, _ = x_sample.shape
    depth = norm1_g.shape[0]
    n_phys = cache_nsa_kv.shape[1]
    n_mem = mem_prompt.shape[1]
    fdim = ffn_w_out.shape[1]
    ret_heads = state_ret.shape[2]
    tok_w = ret_heads * RET_HEAD_DIM
    past = page_table.shape[1] * PAGE
    wb = state_nsa_win.shape[2]
    keep_p = min(WINDOW, s_len)
    assert cache_nsa_kv.shape[2] == PAGE and tok_w == NSA_Q_W and wb == WINDOW
    assert s_len % RET_CHUNK == 0 and t_len <= RET_CHUNK and t_len % 8 == 0

    xp = x_prompt.reshape(bp * s_len, d)
    xs = x_sample.reshape(db * t_len, d)
    cache = cache_nsa_kv.reshape(-1, KV_PLANES, HEAD_DIM)
    win_state = state_nsa_win.reshape(-1, wb, 2 * NSA_KVW)
    mem2 = mem_prompt.reshape(bp * n_mem, d)
    prompt_pages = jnp.arange(bp * s_len // PAGE, dtype=jnp.int32)
    pos_p = jnp.arange(s_len, dtype=jnp.int32)
    pos_s = past + jnp.minimum(jnp.arange(RET_CHUNK, dtype=jnp.int32), t_len - 1)
    ret_tab_p = _retention_tables(ret_heads, RET_CHUNK, RET_CHUNK, pos_p)
    ret_tab_s = _retention_tables(ret_heads, RET_CHUNK, t_len, pos_s)
    zero_state = jnp.zeros((bp, ret_heads, RET_HEAD_DIM, RET_HEAD_DIM), F32)
    zero_conv = jnp.zeros((bp, CONV_W - 1, fdim), F32)

    kv_p, kv_s, win_p, win_s, ret_p, ret_s, conv_p, conv_s, mem_p = ([] for _ in range(9))
    for i in range(depth):
        mem_kv_p = _norm_matmul(mem2, mem_norm_g[i], w_mem_kv[i].astype(BF16), n_mem, PROJ_TN)
        mem_kv_p = mem_kv_p.reshape(bp, n_mem, 2 * MEM_W)
        mem_p.append(mem_kv_p.reshape(bp, n_mem, 2, MEM_HEADS, HEAD_DIM))
        mem_kv_s = cache_mem_kv[i].reshape(db, n_mem, 2 * MEM_W)
        if i % 2 == 0:
            a = i // 2
            w_in = _prep_nsa_w(nsa_w_in[a])
            pe, w1, w2 = nsa_cmp_pe[a], nsa_cmp_w1[a], nsa_cmp_w2[a]
            proj_p = _norm_matmul(xp, norm1_g[i], w_in, PROJ_TM, PROJ_TN)
            proj_s = _norm_matmul(xs, norm1_g[i], w_in, db * t_len, PROJ_TN)
            p3 = proj_p.reshape(bp, s_len, NSA_PROJ_W)
            s3 = proj_s.reshape(db, t_len, NSA_PROJ_W)
            rows_lo, rows_hi = NSA_Q_W, NSA_Q_W + NSA_ROWS_W
            kv_p.append(p3[:, :, rows_lo:rows_hi].reshape(bp, s_len, 4, NSA_KV, HEAD_DIM))
            kv_s.append(s3[:, :, rows_lo:rows_hi].reshape(db, t_len, 4, NSA_KV, HEAD_DIM))
            cmp = _even_odd(_compress(proj_p, prompt_pages, pe, w1, w2, CMP_PAGES, col0=NSA_Q_W), bp,
                            planes_first=True)
            tok_p = _nsa_prompt(p3, cmp[0], cmp[1], NSA_COLS, NSA_TQ, NSA_TK)
            page_ids = page_table + a * n_phys
            cmp = _even_odd(_compress(cache, page_ids.reshape(-1), pe, w1, w2, CMP_PAGES), db)
            tok_s = _nsa_sample(s3, cache, page_ids, cmp[0], cmp[1], win_state, a * db, past, SAMPLE_PAGES)
            qm_block = NSA_MAIN_W // MEM_W
            win_p.append(p3[:, s_len - keep_p:, rows_hi:NSA_MAIN_W].reshape(bp, keep_p, 2, NSA_KV, HEAD_DIM))
            new_win = s3[:, :, rows_hi:NSA_MAIN_W].reshape(db, t_len, 2, NSA_KV, HEAD_DIM)
            win_s.append(jnp.concatenate([state_nsa_win[a], new_win], axis=1)[:, -wb:])
        else:
            bl = i // 2
            w_in = ret_w_in[bl].astype(BF16)
            proj_p = _norm_matmul(xp, norm1_g[i], w_in, PROJ_TM, PROJ_TN)
            proj_s = _norm_matmul(xs, norm1_g[i], w_in, db * t_len, PROJ_TN)
            p3 = proj_p.reshape(bp, s_len, -1)
            s3 = proj_s.reshape(db, t_len, -1)
            tok_p, sp = _retention(p3, ret_gn_g[bl], zero_state, *ret_tab_p)
            s3_pad = jnp.pad(s3, ((0, 0), (0, RET_CHUNK - t_len), (0, 0)))
            tok_s, ss = _retention(s3_pad, ret_gn_g[bl], state_ret[bl], *ret_tab_s)
            tok_s = tok_s[:, :t_len]
            ret_p.append(sp)
            ret_s.append(ss)
            qm_block = 4 * tok_w // MEM_W
        mem_out_p = _mem_attend(p3, qm_block, mem_kv_p, MEM_TQ)
        mem_out_s = _mem_attend(s3, qm_block, mem_kv_s, t_len)
        wo = w_o[i].astype(BF16)
        xp = _wo(xp, tok_p.reshape(bp * s_len, tok_w), mem_out_p.reshape(bp * s_len, MEM_W), wo[:tok_w], wo[tok_w:],
                 WO_TM, WO_TN)
        xs = _wo(xs, tok_s.reshape(db * t_len, tok_w), mem_out_s.reshape(db * t_len, MEM_W), wo[:tok_w], wo[tok_w:],
                 db * t_len, WO_TN)
        last = i == depth - 1
        ffn_w = (norm2_g[i], ffn_w_in[i].astype(BF16), ffn_conv_w[i], ffn_conv_b[i], ffn_w_out[i].astype(BF16))
        xp, tails = _ffn(xp, *ffn_w, zero_conv, final_norm_g, seq_len=s_len, tm=FFN_TM, tf=FFN_TF, tn=FFN_TN,
                         final_norm=last)
        conv_p.append(tails[s_len // FFN_TM - 1::s_len // FFN_TM])
        xs, up = _ffn(xs, *ffn_w, state_ffn_conv[i], final_norm_g, seq_len=t_len, tm=db * t_len, tf=FFN_TF,
                      tn=FFN_TN, final_norm=last)
        conv_s.append(up.reshape(db, t_len, fdim)[:, t_len - (CONV_W - 1):])
    return (xp.reshape(bp, s_len, d), xs.reshape(db, t_len, d), jnp.stack(kv_p), jnp.stack(kv_s), jnp.stack(win_p),
            jnp.stack(win_s), jnp.stack(ret_p), jnp.stack(ret_s), jnp.stack(conv_p), jnp.stack(conv_s),
            jnp.stack(mem_p))
```

```python
import functools

import jax
import jax.numpy as jnp
import numpy as np
from jax import lax
from jax.experimental import pallas as pl
from jax.experimental.pallas import tpu as pltpu

F32 = jnp.float32
BF16 = jnp.bfloat16

HEAD_DIM = 128
MEM_HEADS = 4
NSA_KV = 4
CMP_BLOCK = 32
SLC_BLOCK = 64
TOP_N = 16
WINDOW = 512
FORCE_BONUS = 1.0e4
RET_HEAD_DIM = 256
RET_CHUNK = 128
ROPE_BASE = 10000.0
CONV_W = 3
NORM_EPS = 1e-6
PAGE = 128

V7X_LANES = 128
V7X_VMEM_BYTES = 64 * 1024 * 1024
NEG_BIG = -1e30


def _cparams(n_axes, vmem_bytes):
    limit = min(int(vmem_bytes * 1.25) + (8 << 20), V7X_VMEM_BYTES - (4 << 20))
    return pltpu.CompilerParams(dimension_semantics=("arbitrary",) * n_axes, vmem_limit_bytes=limit)


def _dot(a, b):
    return jnp.dot(a, b, preferred_element_type=F32)


def _dot_nt(a, b):
    return lax.dot_general(a, b, (((1,), (1,)), ((), ())), preferred_element_type=F32)


def _rms(x, g):
    ms = jnp.mean(x * x, axis=-1, keepdims=True)
    return x * lax.rsqrt(ms + NORM_EPS) * g


def _norm_matmul_kernel(x_ref, g_ref, *refs, n_main):
    w_refs, (o_ref, hn_ref) = refs[:-2], refs[-2:]
    j = pl.program_id(1)

    @pl.when(j == 0)
    def _():
        hn_ref[...] = _rms(x_ref[...], g_ref[...]).astype(BF16)

    if len(w_refs) == 1:
        o_ref[...] = _dot(hn_ref[...], w_refs[0][...])
    else:
        @pl.when(j < n_main)
        def _():
            o_ref[...] = _dot(hn_ref[...], w_refs[0][...].astype(BF16))

        @pl.when(j >= n_main)
        def _():
            o_ref[...] = _dot(hn_ref[...], w_refs[1][...])


def _norm_matmul(x, g, w, tm, tn, layer=None, n_main=None, w_tail=None):
    m, k = x.shape
    if w_tail is None:
        n_main, n_tail, weights = w.shape[1] // tn, 0, [w]
        w_specs = [pl.BlockSpec((k, tn), lambda i, j: (0, j))]
    else:
        n_tail, weights = w_tail.shape[1] // tn, [w, w_tail]
        w_specs = [pl.BlockSpec((None, k, tn), lambda i, j: (layer, 0, jnp.minimum(j, n_main - 1))),
                   pl.BlockSpec((k, tn), lambda i, j: (0, jnp.maximum(j - n_main, 0)))]
    vmem = 2 * tm * k * 4 + tm * k * 2 + 2 * (len(weights) + 1) * k * tn * 2 + 2 * tm * tn * 4
    return pl.pallas_call(
        functools.partial(_norm_matmul_kernel, n_main=n_main),
        grid=(m // tm, n_main + n_tail),
        in_specs=[pl.BlockSpec((tm, k), lambda i, j: (i, 0)), pl.BlockSpec((1, k), lambda i, j: (0, 0))] + w_specs,
        out_specs=pl.BlockSpec((tm, tn), lambda i, j: (i, j)),
        out_shape=jax.ShapeDtypeStruct((m, (n_main + n_tail) * tn), F32),
        scratch_shapes=[pltpu.VMEM((tm, k), BF16)],
        compiler_params=_cparams(2, vmem),
        name="norm_matmul",
    )(x, g.reshape(1, k), *weights)


def _wo_kernel(x_ref, tok_ref, mem_ref, wt_ref, wm_ref, o_ref):
    o_ref[...] = (x_ref[...] + _dot(tok_ref[...].astype(BF16), wt_ref[...])
                  + _dot(mem_ref[...].astype(BF16), wm_ref[...]))


def _wo(x, tok, mem, w_tok, w_mem, tm, tn):
    m, d = x.shape
    kt, km = tok.shape[1], mem.shape[1]
    vmem = 2 * (tm * tn * 8 + tm * (kt + km) * 4 + (kt + km) * tn * 2)
    return pl.pallas_call(
        _wo_kernel,
        grid=(m // tm, d // tn),
        in_specs=[
            pl.BlockSpec((tm, tn), lambda i, j: (i, j)),
            pl.BlockSpec((tm, kt), lambda i, j: (i, 0)),
            pl.BlockSpec((tm, km), lambda i, j: (i, 0)),
            pl.BlockSpec((kt, tn), lambda i, j: (0, j)),
            pl.BlockSpec((km, tn), lambda i, j: (0, j)),
        ],
        out_specs=pl.BlockSpec((tm, tn), lambda i, j: (i, j)),
        out_shape=jax.ShapeDtypeStruct((m, d), F32),
        compiler_params=_cparams(2, vmem),
        name="wo_residual",
    )(x, tok, mem, w_tok, w_mem)


def _ffn_kernel(x_ref, g_ref, wa_ref, wg_ref, cw_ref, cb_ref, wo_ref, fg_ref, *refs,
                tiles_per_seq, seq_len, n_f, tn, final_norm):
    multi_seq = tiles_per_seq == 0
    prev_refs, (y_ref, tail_ref, hn_ref, act_ref, carry_ref) = refs[:-5], refs[-5:]
    i = pl.program_id(0)
    s = pl.program_id(1)
    tm, d = x_ref.shape

    @pl.when(s == 0)
    def _():
        hn_ref[...] = _rms(x_ref[...], g_ref[...]).astype(BF16)

    @pl.when(s < n_f)
    def _():
        hn = hn_ref[...]
        a = _dot(hn, wa_ref[...])
        gv = _dot(hn, wg_ref[...])
        row = lax.broadcasted_iota(jnp.int32, a.shape, 0)
        if multi_seq:
            t = row % seq_len
            a1 = jnp.where(t == 0, prev_refs[0][...], pltpu.roll(a, 1, axis=0))
            a2 = jnp.where(t < 2, prev_refs[1][...], pltpu.roll(a, 2, axis=0))
            tail_ref[...] = a
        else:
            prev = jnp.where((i % tiles_per_seq) == 0, prev_refs[0][0], carry_ref[s])
            a1 = jnp.where(row == 0, prev[1:2], pltpu.roll(a, 1, axis=0))
            a2 = jnp.where(row == 0, prev[0:1], jnp.where(row == 1, prev[1:2], pltpu.roll(a, 2, axis=0)))
            carry_ref[s] = a[tm - 2:tm]
            tail_ref[0] = a[tm - 2:tm]
        cw = cw_ref[...]
        ac = cb_ref[...] + a2 * cw[0:1] + a1 * cw[1:2] + a * cw[2:3]
        act_ref[s] = (ac * jax.nn.sigmoid(ac) * gv).astype(BF16)

    for n in range(d // tn):
        @pl.when(s == n_f + n)
        def _(n=n):
            act = jnp.concatenate([act_ref[k] for k in range(n_f)], axis=1)
            cols = slice(n * tn, (n + 1) * tn)
            y_ref[:, cols] = x_ref[:, cols] + _dot(act, wo_ref[...])

    if final_norm:
        @pl.when(s == n_f + d // tn - 1)
        def _():
            y_ref[...] = _rms(y_ref[...], fg_ref[...])


def _ffn(x, norm_g, w_in, conv_w, conv_b, w_out, buf, final_g, *, seq_len, tm, tf, tn, final_norm):
    m, d = x.shape
    fdim = w_out.shape[0]
    nf, nn = fdim // tf, d // tn
    multi_seq = tm > seq_len
    tiles_per_seq = 0 if multi_seq else seq_len // tm
    fa = lambda s: jnp.minimum(s, nf - 1)
    fb = lambda s: jnp.maximum(s - nf, 0)
    if multi_seq:
        n_seq = m // seq_len
        first = jnp.zeros((n_seq, seq_len, fdim), F32).at[:, 0].set(buf[:, 1])
        both = first.at[:, 0].set(buf[:, 0]).at[:, 1].set(buf[:, 1])
        prev = [first.reshape(m, fdim), both.reshape(m, fdim)]
        prev_specs = [pl.BlockSpec((tm, tf), lambda i, s: (i, fa(s)))] * 2
        tail_spec = pl.BlockSpec((tm, tf), lambda i, s: (i, fa(s)))
        tail_shape = jax.ShapeDtypeStruct((m, fdim), F32)
    else:
        prev = [buf]
        prev_specs = [pl.BlockSpec((1, CONV_W - 1, tf), lambda i, s: (i // tiles_per_seq, 0, fa(s)))]
        tail_spec = pl.BlockSpec((1, CONV_W - 1, tf), lambda i, s: (i, 0, fa(s)))
        tail_shape = jax.ShapeDtypeStruct((m // tm, CONV_W - 1, fdim), F32)
    vmem = (2 * tm * d * 4 + tm * d * 2 + 2 * tm * fdim * 2 + 2 * (2 * d * tf * 2 + fdim * tn * 2)
            + 10 * tm * tf * 4 + nf * 8 * tf * 4)
    kern = functools.partial(_ffn_kernel, tiles_per_seq=tiles_per_seq, seq_len=seq_len, n_f=nf, tn=tn,
                             final_norm=final_norm)
    return pl.pallas_call(
        kern,
        grid=(m // tm, nf + nn),
        in_specs=[
            pl.BlockSpec((tm, d), lambda i, s: (i, 0), pipeline_mode=pl.Buffered(1)),
            pl.BlockSpec((1, d), lambda i, s: (0, 0)),
            pl.BlockSpec((d, tf), lambda i, s: (0, fa(s))),
            pl.BlockSpec((d, tf), lambda i, s: (0, fa(s) + nf)),
            pl.BlockSpec((CONV_W, tf), lambda i, s: (0, fa(s))),
            pl.BlockSpec((1, tf), lambda i, s: (0, fa(s))),
            pl.BlockSpec((fdim, tn), lambda i, s: (0, fb(s))),
            pl.BlockSpec((1, d), lambda i, s: (0, 0)),
        ] + prev_specs,
        out_specs=[pl.BlockSpec((tm, d), lambda i, s: (i, 0), pipeline_mode=pl.Buffered(1)), tail_spec],
        out_shape=[jax.ShapeDtypeStruct((m, d), F32), tail_shape],
        scratch_shapes=[pltpu.VMEM((tm, d), BF16), pltpu.VMEM((nf, tm, tf), BF16),
                        pltpu.VMEM((nf, CONV_W - 1, tf), F32)],
        compiler_params=_cparams(2, vmem),
        name="conv_ffn",
    )(x, norm_g.reshape(1, d), w_in, w_in, conv_w, conv_b.reshape(1, fdim), w_out, final_g.reshape(1, d), *prev)


def _mem_kernel(q_ref, kv_ref, o_ref):
    q = q_ref[0] * (HEAD_DIM ** -0.5)
    kv = kv_ref[0]
    width = MEM_HEADS * HEAD_DIM
    outs = []
    for h in range(MEM_HEADS):
        lo = h * HEAD_DIM
        qh = q[:, lo:lo + HEAD_DIM].astype(BF16)
        kh = kv[:, lo:lo + HEAD_DIM].astype(BF16)
        vh = kv[:, width + lo:width + lo + HEAD_DIM].astype(BF16)
        s = _dot_nt(qh, kh)
        e = jnp.exp(s - jnp.max(s, axis=-1, keepdims=True))
        p = e / jnp.sum(e, axis=-1, keepdims=True)
        outs.append(_dot(p.astype(BF16), vh))
    o_ref[0] = jnp.concatenate(outs, axis=-1)


def _mem_attend(proj, qm_col_block, mem_kv, tq):
    b, t, _ = proj.shape
    width = MEM_HEADS * HEAD_DIM
    n_mem = mem_kv.shape[1]
    vmem = 2 * (2 * tq * width * 4 + n_mem * 2 * width * 4) + 8 * tq * n_mem * 4
    return pl.pallas_call(
        _mem_kernel,
        grid=(b, t // tq),
        in_specs=[
            pl.BlockSpec((1, tq, width), lambda bi, i: (bi, i, qm_col_block)),
            pl.BlockSpec((1, n_mem, 2 * width), lambda bi, i: (bi, 0, 0)),
        ],
        out_specs=pl.BlockSpec((1, tq, width), lambda bi, i: (bi, i, 0)),
        out_shape=jax.ShapeDtypeStruct((b, t, width), F32),
        compiler_params=_cparams(2, vmem),
        name="mem_attend",
    )(proj, mem_kv)


def _ret_kernel(q_ref, k_ref, v_ref, gate_ref, cos_ref, sin_ref, dmat_ref, qd_ref, kd_ref, cd_ref,
                gn_ref, s0_ref, tok_ref, s_out_ref, s_ref, *, n_chunks):
    c = pl.program_id(1)
    n_heads, dk = s_ref.shape[0], s_ref.shape[1]

    @pl.when(c == 0)
    def _():
        s_ref[...] = s0_ref[0]

    half = dk // 2
    cos = cos_ref[...]
    sin = sin_ref[...]
    rows = q_ref.shape[1]
    c_len = cos.shape[0]

    def load(ref, h):
        x = ref[0, :, h * dk:(h + 1) * dk]
        return x if rows == c_len else jnp.concatenate([x, jnp.zeros((c_len - rows, dk), F32)], axis=0)

    def rot(x):
        x1, x2 = x[:, :half], x[:, half:]
        return jnp.concatenate([x1 * cos - x2 * sin, x1 * sin + x2 * cos], axis=-1)

    for h in range(n_heads):
        q = rot(load(q_ref, h))
        k = rot(load(k_ref, h)) * (dk ** -0.5)
        v = load(v_ref, h).astype(BF16)
        qb = q.astype(BF16)
        s_old = s_ref[h]
        inner = _dot_nt(qb, k.astype(BF16)) * dmat_ref[h]
        o = _dot(inner.astype(BF16), v) + _dot(qb, s_old.astype(BF16)) * qd_ref[h]
        kd = (k * kd_ref[h]).astype(BF16)
        s_ref[h] = s_old * cd_ref[h] + _dot(kd.T, v)

        mu = jnp.mean(o, axis=-1, keepdims=True)
        dev = o - mu
        var = jnp.mean(dev * dev, axis=-1, keepdims=True)
        y = dev * lax.rsqrt(var + NORM_EPS) * gn_ref[h]
        gate = gate_ref[0, :, h * dk:(h + 1) * dk]
        tok_ref[0, :, h * dk:(h + 1) * dk] = gate * jax.nn.sigmoid(gate) * y[:rows]

    @pl.when(c == n_chunks - 1)
    def _():
        s_out_ref[0] = s_ref[...]


def _retention(proj, gn_g, s0, cos, sin, dmat, q_decay, k_decay, c_decay):
    b, t, _ = proj.shape
    h = s0.shape[1]
    dk = RET_HEAD_DIM
    c = dmat.shape[1]
    rows = min(t, c)
    n = t // rows
    vmem = 2 * (5 * rows * h * dk * 4 + h * c * c * 4 + 2 * h * c * 128 * 4 + 2 * h * dk * dk * 4) + 3 * h * dk * dk * 4
    blk = lambda group: pl.BlockSpec((1, rows, h * dk), lambda bi, ci: (bi, ci, group))
    whole = lambda shape: pl.BlockSpec(shape, lambda bi, ci: (0,) * len(shape))
    return pl.pallas_call(
        functools.partial(_ret_kernel, n_chunks=n),
        grid=(b, n),
        in_specs=[
            blk(0), blk(1), blk(2), blk(3),
            pl.BlockSpec((c, dk // 2), lambda bi, ci: (ci, 0)),
            pl.BlockSpec((c, dk // 2), lambda bi, ci: (ci, 0)),
            whole((h, c, c)), whole((h, c, 1)), whole((h, c, 1)), whole((h, 1, dk)), whole((h, 1, dk)),
            pl.BlockSpec((1, h, dk, dk), lambda bi, ci: (bi, 0, 0, 0)),
        ],
        out_specs=[
            pl.BlockSpec((1, rows, h * dk), lambda bi, ci: (bi, ci, 0)),
            pl.BlockSpec((1, h, dk, dk), lambda bi, ci: (bi, 0, 0, 0)),
        ],
        out_shape=[jax.ShapeDtypeStruct((b, t, h * dk), F32),
                   jax.ShapeDtypeStruct((b, h, dk, dk), F32)],
        scratch_shapes=[pltpu.VMEM((h, dk, dk), F32)],
        compiler_params=_cparams(2, vmem),
        name="retention",
    )(proj, proj, proj, proj, cos, sin, dmat, q_decay, k_decay, c_decay, gn_g.reshape(h, 1, dk), s0)


def _retention_tables(n_heads, c_pad, c_real, pos):
    log_g = jnp.log1p(-jnp.exp2(-5.0 - jnp.arange(n_heads, dtype=F32)))
    i = jnp.arange(c_pad, dtype=F32)
    live = i < c_real
    diff = i[:, None] - i[None, :]
    dmat = jnp.where((diff >= 0) & live[:, None] & live[None, :],
                     jnp.exp(jnp.maximum(diff, 0.0)[None] * log_g[:, None, None]), 0.0)
    q_decay = jnp.exp((i + 1.0)[None, :] * log_g[:, None])[..., None]
    k_decay = jnp.where(live[None, :], jnp.exp((c_real - 1.0 - i)[None, :] * log_g[:, None]), 0.0)[..., None]
    c_decay = jnp.broadcast_to(jnp.exp(c_real * log_g)[:, None, None], (n_heads, 1, RET_HEAD_DIM))
    half = RET_HEAD_DIM // 2
    inv = ROPE_BASE ** (-jnp.arange(half, dtype=F32) / half)
    ang = pos.astype(F32)[:, None] * inv[None, :]
    return jnp.cos(ang), jnp.sin(ang), dmat, q_decay, k_decay, c_decay


KV_PLANES = 4 * NSA_KV
CMP_PLANES = 2 * NSA_KV


def _compress_kernel(plist_ref, src_ref, pe_ref, w1_ref, w2_ref, o_ref, buf_ref, lhs_ref, sem_ref, *,
                     pages, n_steps, col0):
    s = pl.program_id(0)
    slot = s % 2
    bpp = PAGE // CMP_BLOCK
    nblk = pages * bpp
    rows = nblk * CMP_PLANES

    def copies(step, to_slot):
        cps = []
        for p in range(pages):
            tok0 = plist_ref[step * pages + p] * PAGE
            for n in range(bpp):
                toks = pl.ds(tok0 + n * CMP_BLOCK, CMP_BLOCK)
                if col0 is None:
                    src = src_ref.at[toks, pl.ds(0, CMP_PLANES), :]
                else:
                    src = src_ref.at[toks, pl.ds(col0, CMP_PLANES * HEAD_DIM)]
                cps.append(pltpu.make_async_copy(src, buf_ref.at[to_slot, :, p * bpp + n], sem_ref.at[to_slot]))
        return cps

    @pl.when(s == 0)
    def _():
        for cp in copies(0, 0):
            cp.start()

    @pl.when(s + 1 < n_steps)
    def _():
        for cp in copies(s + 1, 1 - slot):
            cp.start()

    for cp in copies(s, slot):
        cp.wait()

    row = lax.broadcasted_iota(jnp.int32, (rows, HEAD_DIM), 0)
    if col0 is None:
        for c in range(CMP_BLOCK):
            x = buf_ref[slot, c] + pe_ref[c]
            lhs_ref[:, c * HEAD_DIM:(c + 1) * HEAD_DIM] = x.reshape(rows, HEAD_DIM).astype(BF16)
        is_key = (row % CMP_PLANES) < NSA_KV
    else:
        for c in range(CMP_BLOCK):
            x = buf_ref[slot, c]
            for j in range(CMP_PLANES):
                xj = x[:, j * HEAD_DIM:(j + 1) * HEAD_DIM] + pe_ref[c, j:j + 1, :]
                lhs_ref[j * nblk:(j + 1) * nblk, c * HEAD_DIM:(c + 1) * HEAD_DIM] = xj.astype(BF16)
        is_key = row < NSA_KV * nblk
    h = _dot(lhs_ref[...], w1_ref[...])
    h = jax.nn.gelu(jnp.where(is_key, h[:, :HEAD_DIM], h[:, HEAD_DIM:]))
    out = _dot(h.astype(BF16), w2_ref[...])
    o_ref[...] = jnp.where(is_key, out[:, :HEAD_DIM], out[:, HEAD_DIM:]).reshape(o_ref.shape)


def _compress(src, plist, pe, w1, w2, pages, col0=None):
    n_pages = plist.shape[0]
    nblk = pages * (PAGE // CMP_BLOCK)
    n_total = n_pages * (PAGE // CMP_BLOCK)
    rows = nblk * CMP_PLANES
    k_dim = CMP_BLOCK * HEAD_DIM
    pe8 = jnp.repeat(pe.transpose(1, 0, 2), NSA_KV, axis=1)
    w1b = jnp.concatenate([w1[0], w1[1]], axis=1).astype(BF16)
    w2b = jnp.concatenate([w2[0], w2[1]], axis=1).astype(BF16)
    if col0 is None:
        buf_shape = (2, CMP_BLOCK, nblk, CMP_PLANES, HEAD_DIM)
        out_spec = pl.BlockSpec((nblk, CMP_PLANES, HEAD_DIM), lambda s, pr: (s, 0, 0))
        out_shape = (n_total, CMP_PLANES, HEAD_DIM)
    else:
        buf_shape = (2, CMP_BLOCK, nblk, CMP_PLANES * HEAD_DIM)
        out_spec = pl.BlockSpec((CMP_PLANES, nblk, HEAD_DIM), lambda s, pr: (0, s, 0))
        out_shape = (CMP_PLANES, n_total, HEAD_DIM)
    vmem = 2 * CMP_BLOCK * rows * HEAD_DIM * 4 + rows * k_dim * 2 + 2 * k_dim * 2 * HEAD_DIM * 2 + 8 * rows * HEAD_DIM * 4
    grid_spec = pltpu.PrefetchScalarGridSpec(
        num_scalar_prefetch=1,
        grid=(n_pages // pages,),
        in_specs=[
            pl.BlockSpec(memory_space=pl.ANY),
            pl.BlockSpec((CMP_BLOCK, CMP_PLANES, HEAD_DIM), lambda s, pr: (0, 0, 0)),
            pl.BlockSpec((k_dim, 2 * HEAD_DIM), lambda s, pr: (0, 0)),
            pl.BlockSpec((HEAD_DIM, 2 * HEAD_DIM), lambda s, pr: (0, 0)),
        ],
        out_specs=out_spec,
        scratch_shapes=[pltpu.VMEM(buf_shape, F32), pltpu.VMEM((rows, k_dim), BF16), pltpu.SemaphoreType.DMA((2,))],
    )
    return pl.pallas_call(
        functools.partial(_compress_kernel, pages=pages, n_steps=n_pages // pages, col0=col0),
        grid_spec=grid_spec,
        out_shape=jax.ShapeDtypeStruct(out_shape, F32),
        compiler_params=_cparams(1, vmem),
        name="nsa_compress",
    )(plist, src, pe8, w1b, w2b)


def _even_odd(c, n_seq, planes_first=False):
    d = c.shape[-1]
    nc = c.shape[1 if planes_first else 0] // n_seq
    if planes_first:
        c = c.reshape(2, NSA_KV, n_seq, nc // 2, 2, d).transpose(0, 1, 2, 4, 3, 5)
    else:
        c = c.reshape(n_seq, nc // 2, 2, 2, NSA_KV, d).transpose(3, 4, 0, 2, 1, 5)
    return c.reshape(2, NSA_KV, n_seq, nc, d).astype(BF16)


def _cmp_attend(qb, kc, vc, qpos_col):
    nc = kc.shape[0]
    lane = lax.broadcasted_iota(jnp.int32, (1, nc), 1)
    blk = jnp.where(lane < nc // 2, 2 * lane, 2 * (lane - nc // 2) + 1)
    vis = (blk * CMP_BLOCK + (CMP_BLOCK - 1)) <= qpos_col
    s = jnp.where(vis, _dot_nt(qb, kc), -jnp.inf)
    m = jnp.max(s, axis=-1, keepdims=True)
    m = jnp.where(m > -jnp.inf, m, 0.0)
    e = jnp.where(vis, jnp.exp(s - m), 0.0)
    p = e / jnp.maximum(jnp.sum(e, axis=-1, keepdims=True), 1e-30)
    return _dot(p.astype(BF16), vc), p


def _select_blocks(p_t, qpos_row, n_sel, m_hi, score_ref):
    nb, nq = p_t.shape
    blk = lax.broadcasted_iota(jnp.int32, (nb, nq), 0)
    valid = blk * SLC_BLOCK <= qpos_row
    cur = qpos_row // SLC_BLOCK
    forced = (blk == 0) | (blk == cur) | (blk == cur - 1)
    score = jnp.where(valid, p_t + jnp.where(forced, FORCE_BONUS, 0.0), -jnp.inf)
    score_ref[...] = score
    rows = min(64, nb)
    outs = []
    for r0 in range(0, nb, rows):
        sc = score[r0:r0 + rows]
        bk = blk[r0:r0 + rows]

        def body(mi, rank, sc=sc, bk=bk):
            other = score_ref[pl.ds(mi, 1), :]
            ge = jnp.where(other >= sc, 1.0, 0.0)
            gt = jnp.where(other > sc, 1.0, 0.0)
            return rank + jnp.where(bk > mi, ge, gt)

        rank = lax.fori_loop(0, m_hi, body, jnp.zeros_like(sc))
        outs.append(jnp.where((rank < n_sel) & valid[r0:r0 + rows], 1.0, 0.0))
    return jnp.concatenate(outs, axis=0) if len(outs) > 1 else outs[0]


def _lane_tile(x, n_lanes):
    reps = n_lanes // x.shape[1]
    return x if reps == 1 else jnp.concatenate([x] * reps, axis=1)


def _softmax_step(s, vis, v, m_ref, l_ref, acc_ref, r0, value_shift=0):
    rows, n_keys = s.shape
    sl = pl.ds(r0, rows)
    s = jnp.where(vis, s, NEG_BIG)
    m_old = m_ref[sl, :]
    m_new = jnp.maximum(m_old, jnp.max(s, axis=-1, keepdims=True))
    alpha = jnp.exp(m_old - m_new)
    p = jnp.exp(s - _lane_tile(m_new, n_keys))
    l_ref[sl, :] = alpha * l_ref[sl, :] + jnp.sum(p, axis=-1, keepdims=True)
    pv = pltpu.roll(p, value_shift, axis=1) if value_shift else p
    acc_ref[sl, :] = alpha * acc_ref[sl, :] + _dot(pv.astype(BF16), v)
    m_ref[sl, :] = m_new


def _softmax_finish(m_ref, l_ref, acc_ref):
    seen = m_ref[...] > 0.5 * NEG_BIG
    return jnp.where(seen, acc_ref[...] / jnp.maximum(l_ref[...], 1e-30), 0.0)


def _softmax_once(s, vis, v):
    groups = s.shape[1] // V7X_LANES
    s = jnp.where(vis, s, NEG_BIG)
    part = lambda x, c: x[:, c * V7X_LANES:(c + 1) * V7X_LANES]
    m = functools.reduce(jnp.maximum, [part(s, c) for c in range(groups)])
    m = jnp.max(m, axis=-1, keepdims=True)
    e = jnp.exp(s - m)
    l = functools.reduce(jnp.add, [part(e, c) for c in range(groups)])
    l = jnp.maximum(jnp.sum(l, axis=-1, keepdims=True), 1e-30)
    return jnp.where(m > 0.5 * NEG_BIG, _dot(e.astype(BF16), v) / l, 0.0)


def _block_expander(n_blocks, key):
    blk = lax.broadcasted_iota(jnp.int32, (n_blocks, key.shape[1]), 0)
    return jnp.where(blk == key // SLC_BLOCK, 1.0, 0.0).astype(BF16)


def _nsa_prompt_kernel(q_ref, kc_ref, vc_ref, ks_ref, vs_ref, kw_ref, vw_ref, gate_ref, tok_ref,
                       score_ref, s_ref, m_ref, l_ref, acc_ref, *, tk):
    i = pl.program_id(2)
    tq = q_ref.shape[1]
    r = q_ref.shape[2] // HEAD_DIM
    nb = kc_ref.shape[2] // (SLC_BLOCK // CMP_BLOCK)
    nbp = -(-nb // V7X_LANES) * V7X_LANES
    t0 = i * tq
    q = q_ref[0] * (HEAD_DIM ** -0.5)
    qb = jnp.concatenate([q[:, h * HEAD_DIM:(h + 1) * HEAD_DIM] for h in range(r)], axis=0).astype(BF16)
    tcol = t0 + lax.broadcasted_iota(jnp.int32, (tq, 1), 0)
    trow = t0 + lax.broadcasted_iota(jnp.int32, (1, tq), 1)

    o_cmp, p = _cmp_attend(qb, kc_ref[0, 0], vc_ref[0, 0], jnp.concatenate([tcol] * r, axis=0))
    p_grp = p[0:tq]
    for h in range(1, r):
        p_grp = p_grp + p[h * tq:(h + 1) * tq]
    p_pair = p_grp + pltpu.roll(p_grp, nb, axis=1)
    m_hi = jnp.minimum((t0 + tq - 1) // SLC_BLOCK + 1, nb)
    sel_t = _select_blocks(p_pair.T[:nb], trow, TOP_N, m_hi, score_ref)
    if nbp > nb:
        sel_t = jnp.concatenate([sel_t, jnp.zeros((nbp - nb, tq), F32)], axis=0)
    sel = sel_t.T.astype(BF16)

    n_kt = (t0 + tq + tk - 1) // tk
    lane_groups = tk // V7X_LANES
    m_ref[...] = jnp.full(m_ref.shape, NEG_BIG, F32)

    def score_tile(j, carry):
        k0 = pl.multiple_of(j * tk, tk)
        s = _dot_nt(qb, ks_ref[0, pl.ds(k0, tk), :].astype(BF16))
        picked = _dot(sel, _block_expander(nbp, k0 + lax.broadcasted_iota(jnp.int32, (1, tk), 1)))
        kpos = k0 + lax.broadcasted_iota(jnp.int32, (tq, tk), 1)
        vis = (picked > 0.5) & (kpos <= tcol)
        for h in range(r):
            rows = pl.ds(h * tq, tq)
            sh = jnp.where(vis, s[h * tq:(h + 1) * tq], NEG_BIG)
            s_ref[j, rows, :] = sh
            m = m_ref[rows, :]
            for c in range(lane_groups):
                m = jnp.maximum(m, sh[:, c * V7X_LANES:(c + 1) * V7X_LANES])
            m_ref[rows, :] = m
        return carry

    lax.fori_loop(0, n_kt, score_tile, 0)
    m_row = jnp.max(m_ref[...], axis=-1, keepdims=True)
    m_ref[...] = jnp.broadcast_to(m_row, m_ref.shape)
    l_ref[...] = jnp.zeros(l_ref.shape, F32)
    acc_ref[...] = jnp.zeros(acc_ref.shape, F32)

    def value_tile(j, carry):
        k0 = pl.multiple_of(j * tk, tk)
        p = jnp.exp(s_ref[j] - _lane_tile(m_ref[...], tk))
        l = l_ref[...]
        for c in range(lane_groups):
            l = l + p[:, c * V7X_LANES:(c + 1) * V7X_LANES]
        l_ref[...] = l
        acc_ref[...] += _dot(p.astype(BF16), vs_ref[0, pl.ds(k0, tk), :].astype(BF16))
        return carry

    lax.fori_loop(0, n_kt, value_tile, 0)
    denom = jnp.maximum(jnp.sum(l_ref[...], axis=-1, keepdims=True), 1e-30)
    o_slc = jnp.where(m_row > 0.5 * NEG_BIG, acc_ref[...] / denom, 0.0)

    span = WINDOW + tq
    w0 = pl.multiple_of(jnp.maximum(t0 + tq - span, 0), tq)
    kw = kw_ref[0, pl.ds(w0, span), :].astype(BF16)
    vw = vw_ref[0, pl.ds(w0, span), :].astype(BF16)
    s = _dot_nt(qb, kw)
    kpos = w0 + lax.broadcasted_iota(jnp.int32, (tq, span), 1)
    vis = (kpos <= tcol) & (kpos > tcol - WINDOW)
    o_win = [_softmax_once(s[h * tq:(h + 1) * tq], vis, vw) for h in range(r)]

    gt = jax.nn.sigmoid(gate_ref[0])
    outs = []
    for h in range(r):
        rows = slice(h * tq, (h + 1) * tq)
        outs.append(gt[:, 3 * h:3 * h + 1] * o_cmp[rows] + gt[:, 3 * h + 1:3 * h + 2] * o_slc[rows]
                    + gt[:, 3 * h + 2:3 * h + 3] * o_win[h])
    tok_ref[0] = jnp.concatenate(outs, axis=-1)


def _nsa_prompt(proj, kc, vc, cols, tq, tk):
    b, t, _ = proj.shape
    g = kc.shape[0]
    nc = kc.shape[2]
    r = 3
    nb = nc // (SLC_BLOCK // CMP_BLOCK)
    assert t >= WINDOW + tq and t % tk == 0 and tk % tq == 0 and nb % 64 == 0 and nc <= V7X_LANES
    kv_spec = lambda c0: pl.BlockSpec((1, t, HEAD_DIM), lambda bi, gi, i: (bi, 0, c0 + gi))
    cmp_spec = pl.BlockSpec((1, 1, nc, HEAD_DIM), lambda bi, gi, i: (gi, bi, 0, 0))
    vmem = 2 * 4 * t * HEAD_DIM * 4 + r * tq * t * 4 + 24 * r * tq * max(tk, WINDOW + tq) * 4
    return pl.pallas_call(
        functools.partial(_nsa_prompt_kernel, tk=tk),
        grid=(b, g, t // tq),
        in_specs=[
            pl.BlockSpec((1, tq, r * HEAD_DIM), lambda bi, gi, i: (bi, i, cols["q"] // r + gi)),
            cmp_spec, cmp_spec,
            kv_spec(cols["ks"]), kv_spec(cols["vs"]), kv_spec(cols["kw"]), kv_spec(cols["vw"]),
            pl.BlockSpec((1, tq, HEAD_DIM), lambda bi, gi, i: (bi, i, cols["gates"] + gi)),
        ],
        out_specs=pl.BlockSpec((1, tq, r * HEAD_DIM), lambda bi, gi, i: (bi, i, gi)),
        out_shape=jax.ShapeDtypeStruct((b, t, g * r * HEAD_DIM), F32),
        scratch_shapes=[pltpu.VMEM((nb, tq), F32), pltpu.VMEM((t // tk, r * tq, tk), F32),
                        pltpu.VMEM((r * tq, V7X_LANES), F32), pltpu.VMEM((r * tq, V7X_LANES), F32),
                        pltpu.VMEM((r * tq, HEAD_DIM), F32)],
        compiler_params=_cparams(3, vmem),
        name="nsa_prompt",
    )(proj, kc, vc, proj, proj, proj, proj, proj)


NSA_HEADS = 12
NSA_KVW = NSA_KV * HEAD_DIM
NSA_Q_W = NSA_HEADS * HEAD_DIM
NSA_ROWS_W = 4 * NSA_KVW
NSA_WIN_W = 2 * NSA_KVW
MEM_W = MEM_HEADS * HEAD_DIM
NSA_GATES = 3 * NSA_HEADS
NSA_MAIN_W = NSA_Q_W + NSA_ROWS_W + NSA_WIN_W
NSA_PROJ_W = NSA_MAIN_W + MEM_W + NSA_KV * V7X_LANES
NSA_COLS = {
    "q": 0,
    "ks": (NSA_Q_W + 2 * NSA_KVW) // V7X_LANES,
    "vs": (NSA_Q_W + 3 * NSA_KVW) // V7X_LANES,
    "kw": (NSA_Q_W + NSA_ROWS_W) // V7X_LANES,
    "vw": (NSA_Q_W + NSA_ROWS_W + NSA_KVW) // V7X_LANES,
    "gates": (NSA_MAIN_W + MEM_W) // V7X_LANES,
}


def _prep_nsa_w(w):
    d = w.shape[0]
    gates = w[:, NSA_MAIN_W:NSA_MAIN_W + NSA_GATES].reshape(d, NSA_KV, NSA_GATES // NSA_KV)
    gates = jnp.pad(gates, ((0, 0), (0, 0), (0, V7X_LANES - NSA_GATES // NSA_KV))).reshape(d, NSA_KV * V7X_LANES)
    return jnp.concatenate([w[:, NSA_MAIN_W + NSA_GATES:], gates], axis=1).astype(BF16)


def _stack_heads(q, n_heads):
    return jnp.concatenate([q[:, h * HEAD_DIM:(h + 1) * HEAD_DIM] for h in range(n_heads)], axis=0)


def _nsa_sample_cmp_kernel(q_ref, kc_ref, vc_ref, ocmp_ref, pslc_ref, *, past):
    t = q_ref.shape[1]
    g_n = kc_ref.shape[0]
    r = q_ref.shape[2] // HEAD_DIM // g_n
    nb = kc_ref.shape[2] // (SLC_BLOCK // CMP_BLOCK)
    q = q_ref[0] * (HEAD_DIM ** -0.5)
    qpos = past + lax.broadcasted_iota(jnp.int32, (t, 1), 0)
    qpos_r = jnp.concatenate([qpos] * r, axis=0)
    for g in range(g_n):
        qb = _stack_heads(q[:, g * r * HEAD_DIM:(g + 1) * r * HEAD_DIM], r).astype(BF16)
        o, p = _cmp_attend(qb, kc_ref[g, 0], vc_ref[g, 0], qpos_r)
        ocmp_ref[0, g * r * t:(g + 1) * r * t, :] = o
        p_grp = p[0:t]
        for h in range(1, r):
            p_grp = p_grp + p[h * t:(h + 1) * t]
        p_pair = p_grp + pltpu.roll(p_grp, nb, axis=1)
        pslc_ref[0, g * t:(g + 1) * t, :] = p_pair[:, :nb]


def _nsa_sample_cmp(proj, kc, vc, past):
    b, t, _ = proj.shape
    g, _, nc, _ = kc.shape
    nb = nc // (SLC_BLOCK // CMP_BLOCK)
    cmp_spec = pl.BlockSpec((g, 1, nc, HEAD_DIM), lambda bi: (0, bi, 0, 0))
    vmem = 2 * (t * NSA_Q_W * 4 + 2 * g * nc * HEAD_DIM * 2) + 64 * nc * 4 * 8
    return pl.pallas_call(
        functools.partial(_nsa_sample_cmp_kernel, past=past),
        grid=(b,),
        in_specs=[pl.BlockSpec((1, t, NSA_Q_W), lambda bi: (bi, 0, 0)), cmp_spec, cmp_spec],
        out_specs=[pl.BlockSpec((1, NSA_HEADS * t, HEAD_DIM), lambda bi: (bi, 0, 0)),
                   pl.BlockSpec((1, g * t, nb), lambda bi: (bi, 0, 0))],
        out_shape=[jax.ShapeDtypeStruct((b, NSA_HEADS * t, HEAD_DIM), F32),
                   jax.ShapeDtypeStruct((b, g * t, nb), F32)],
        compiler_params=_cparams(1, vmem),
        name="nsa_sample_cmp",
    )(proj, kc, vc)


def _rank_kernel(p_ref, qpos_ref, sel_ref, score_ref, *, n_sel):
    sel_ref[...] = _select_blocks(p_ref[...], qpos_ref[...], n_sel, p_ref.shape[0], score_ref)


def _rank_blocks(p_t, qpos, n_sel):
    nb, nq = p_t.shape
    tq = V7X_LANES
    return pl.pallas_call(
        functools.partial(_rank_kernel, n_sel=n_sel),
        grid=(nq // tq,),
        in_specs=[pl.BlockSpec((nb, tq), lambda i: (0, i)), pl.BlockSpec((1, tq), lambda i: (0, i))],
        out_specs=pl.BlockSpec((nb, tq), lambda i: (0, i)),
        out_shape=jax.ShapeDtypeStruct((nb, nq), F32),
        scratch_shapes=[pltpu.VMEM((nb, tq), F32)],
        compiler_params=_cparams(1, 8 * nb * tq * 4),
        name="nsa_rank",
    )(p_t, qpos)


def _nsa_sample_attend_kernel(pages_ref, *refs, past, n_steps, pps):
    kv_refs = refs[:pps]
    (q_ref, sel_ref, ocmp_ref, ksn_ref, vsn_ref, kwn_ref, vwn_ref, wst_ref, gate_ref, tok_ref,
     qs_ref, m_ref, l_ref, acc_ref) = refs[pps:]
    j = pl.program_id(1)
    t = q_ref.shape[1]
    g_n = NSA_KV
    r = NSA_HEADS // g_n
    rg = r * t
    nb = sel_ref.shape[2]
    span = pps * PAGE

    @pl.when(j == 0)
    def _():
        qs_ref[...] = _stack_heads(q_ref[0] * (HEAD_DIM ** -0.5), NSA_HEADS)
        m_ref[...] = jnp.full(m_ref.shape, NEG_BIG, F32)
        l_ref[...] = jnp.zeros(l_ref.shape, F32)
        acc_ref[...] = jnp.zeros(acc_ref.shape, F32)

    half = KV_PLANES // 2
    cols = span * half
    kvb = jnp.concatenate([kv[...].reshape(PAGE * half, HEAD_DIM) for kv in kv_refs], axis=0).astype(BF16)
    s = _dot_nt(qs_ref[...].astype(BF16), kvb)
    lane = lax.broadcasted_iota(jnp.int32, (g_n * t, cols), 1)
    own = lax.broadcasted_iota(jnp.int32, (g_n * t, cols), 0) // t
    picked = _dot(sel_ref[0], _block_expander(nb, j * span + lane[0:1] // half))
    hit = jnp.where((picked > 0.5) & ((lane % half) == own), 1.0, 0.0)
    vis = jnp.concatenate([hit[g * t:(g + 1) * t] for g in range(g_n) for _ in range(r)], axis=0) > 0.5
    _softmax_step(s, vis, kvb, m_ref, l_ref, acc_ref, 0, value_shift=g_n)

    @pl.when(j == n_steps - 1)
    def _():
        tcol = jnp.concatenate([lax.broadcasted_iota(jnp.int32, (t, 1), 0)] * r, axis=0)
        pad = jnp.zeros((PAGE - t, HEAD_DIM), F32)

        def new_keys(ref, g):
            return jnp.concatenate([ref[0][:, g * HEAD_DIM:(g + 1) * HEAD_DIM], pad], axis=0)

        lane = lax.broadcasted_iota(jnp.int32, (rg, PAGE), 1)
        for g in range(g_n):
            qg = qs_ref[g * rg:(g + 1) * rg, :].astype(BF16)
            _softmax_step(_dot_nt(qg, new_keys(ksn_ref, g).astype(BF16)), lane <= tcol,
                          new_keys(vsn_ref, g).astype(BF16), m_ref, l_ref, acc_ref, g * rg)
        o_slc = _softmax_finish(m_ref, l_ref, acc_ref)

        wb = wst_ref.shape[1]
        kpos = past - wb + lax.broadcasted_iota(jnp.int32, (rg, wb + PAGE), 1)
        qpos = past + tcol
        vis = (kpos <= qpos) & (kpos > qpos - WINDOW)
        wst = wst_ref[0]
        gates = gate_ref[0]
        ocmp = ocmp_ref[0]
        outs = []
        for g in range(g_n):
            qg = qs_ref[g * rg:(g + 1) * rg, :].astype(BF16)
            kw = jnp.concatenate([wst[:, g * HEAD_DIM:(g + 1) * HEAD_DIM], new_keys(kwn_ref, g)], axis=0)
            vw = jnp.concatenate([wst[:, NSA_KVW + g * HEAD_DIM:NSA_KVW + (g + 1) * HEAD_DIM],
                                  new_keys(vwn_ref, g)], axis=0)
            o_win = _softmax_once(_dot_nt(qg, kw.astype(BF16)), vis, vw.astype(BF16))
            gt = jax.nn.sigmoid(gates[:, g * V7X_LANES:(g + 1) * V7X_LANES])
            for h in range(r):
                rows = slice(g * rg + h * t, g * rg + (h + 1) * t)
                outs.append(gt[:, 3 * h:3 * h + 1] * ocmp[rows] + gt[:, 3 * h + 1:3 * h + 2] * o_slc[rows]
                            + gt[:, 3 * h + 2:3 * h + 3] * o_win[h * t:(h + 1) * t])
        tok_ref[0] = jnp.concatenate(outs, axis=-1)


def _nsa_sample_attend(cache, pages, proj, sel, ocmp, win_state, win_index0, past, pps):
    b, t, _ = proj.shape
    npg = pages.shape[0] // b
    nb = sel.shape[2]
    wb = win_state.shape[1]
    n_steps = npg // pps
    assert nb * SLC_BLOCK == npg * PAGE == past and t <= SLC_BLOCK and npg % pps == 0
    col = lambda first: first * V7X_LANES // NSA_KVW
    new_spec = lambda name: pl.BlockSpec((1, t, NSA_KVW), lambda bi, j, pg: (bi, 0, col(NSA_COLS[name])))
    page_spec = lambda k: pl.BlockSpec((PAGE, KV_PLANES // 2, HEAD_DIM),
                                       lambda bi, j, pg: (pg[bi * npg + j * pps + k], 1, 0))
    rows = NSA_HEADS * t
    vmem = (2 * (pps * PAGE * NSA_KVW * 2 * 4 + wb * 2 * NSA_KVW * 4 + t * NSA_PROJ_W * 4)
            + 6 * pps * PAGE * HEAD_DIM * 4 + 8 * (wb + PAGE) * 128 * 4)
    grid_spec = pltpu.PrefetchScalarGridSpec(
        num_scalar_prefetch=1,
        grid=(b, n_steps),
        in_specs=[page_spec(k) for k in range(pps)] + [
            pl.BlockSpec((1, t, NSA_Q_W), lambda bi, j, pg: (bi, 0, 0)),
            pl.BlockSpec((1, NSA_KV * t, nb), lambda bi, j, pg: (bi, 0, 0)),
            pl.BlockSpec((1, rows, HEAD_DIM), lambda bi, j, pg: (bi, 0, 0)),
            new_spec("ks"), new_spec("vs"), new_spec("kw"), new_spec("vw"),
            pl.BlockSpec((1, wb, 2 * NSA_KVW), lambda bi, j, pg: (win_index0 + bi, 0, 0)),
            pl.BlockSpec((1, t, NSA_KV * V7X_LANES), lambda bi, j, pg: (bi, 0, col(NSA_COLS["gates"]))),
        ],
        out_specs=pl.BlockSpec((1, t, NSA_Q_W), lambda bi, j, pg: (bi, 0, 0)),
        scratch_shapes=[pltpu.VMEM((rows, HEAD_DIM), F32), pltpu.VMEM((rows, V7X_LANES), F32),
                        pltpu.VMEM((rows, V7X_LANES), F32), pltpu.VMEM((rows, HEAD_DIM), F32)],
    )
    return pl.pallas_call(
        functools.partial(_nsa_sample_attend_kernel, past=past, n_steps=n_steps, pps=pps),
        grid_spec=grid_spec,
        out_shape=jax.ShapeDtypeStruct((b, t, NSA_Q_W), F32),
        compiler_params=_cparams(2, vmem),
        name="nsa_sample_attend",
    )(pages, *([cache] * pps), proj, sel, ocmp, proj, proj, proj, proj, win_state, proj)


def _nsa_sample(proj, cache, page_ids, kc, vc, win_state, win_index0, past, pps):
    b, t, _ = proj.shape
    ocmp, pslc = _nsa_sample_cmp(proj, kc, vc, past)
    nb = pslc.shape[2]
    qpos = jnp.broadcast_to(past + jnp.arange(t, dtype=jnp.int32), (b * NSA_KV, t)).reshape(1, -1)
    sel_t = _rank_blocks(pslc.reshape(b * NSA_KV * t, nb).T, qpos, TOP_N - 1)
    sel = sel_t.T.reshape(b, NSA_KV * t, nb).astype(BF16)
    return _nsa_sample_attend(cache, page_ids.reshape(-1), proj, sel, ocmp, win_state, win_index0, past, pps)


PROJ_TM = 1024
PROJ_TN = 512
WO_TM = 512
FFN_TM = 512
FFN_TF = 512
FFN_TN = 512
WO_TN = 1024
MEM_TQ = 512
NSA_TQ = 256
NSA_TK = 512
CMP_PAGES = 16
SAMPLE_PAGES = 8


def kernel(x_prompt, x_sample, cache_nsa_kv, state_nsa_win, state_ret, state_ffn_conv, cache_mem_kv, page_table,
           mem_prompt, norm1_g, nsa_w_in, nsa_cmp_pe, nsa_cmp_w1, nsa_cmp_w2, ret_w_in, ret_gn_g, mem_norm_g,
           w_mem_kv, w_o, norm2_g, ffn_w_in, ffn_conv_w, ffn_conv_b, ffn_w_out, final_norm_g):
    bp, s_len, d = x_prompt.shape
    db, t_len, _ = x_sample.shape
    depth = norm1_g.shape[0]
    n_phys = cache_nsa_kv.shape[1]
    n_mem = mem_prompt.shape[1]
    fdim = ffn_w_out.shape[1]
    ret_heads = state_ret.shape[2]
    tok_w = ret_heads * RET_HEAD_DIM
    past = page_table.shape[1] * PAGE
    wb = state_nsa_win.shape[2]
    keep_p = min(WINDOW, s_len)
    assert cache_nsa_kv.shape[2] == PAGE and tok_w == NSA_Q_W and wb == WINDOW
    assert s_len % RET_CHUNK == 0 and t_len <= RET_CHUNK and t_len % 8 == 0

    xp = x_prompt.reshape(bp * s_len, d)
    xs = x_sample.reshape(db * t_len, d)
    cache = cache_nsa_kv.reshape(-1, KV_PLANES, HEAD_DIM)
    win_state = state_nsa_win.reshape(-1, wb, 2 * NSA_KVW)
    mem2 = mem_prompt.reshape(bp * n_mem, d)
    prompt_pages = jnp.arange(bp * s_len // PAGE, dtype=jnp.int32)
    pos_p = jnp.arange(s_len, dtype=jnp.int32)
    pos_s = past + jnp.minimum(jnp.arange(RET_CHUNK, dtype=jnp.int32), t_len - 1)
    ret_tab_p = _retention_tables(ret_heads, RET_CHUNK, RET_CHUNK, pos_p)
    ret_tab_s = _retention_tables(ret_heads, RET_CHUNK, t_len, pos_s)
    zero_state = jnp.zeros((bp, ret_heads, RET_HEAD_DIM, RET_HEAD_DIM), F32)
    zero_conv = jnp.zeros((bp, CONV_W - 1, fdim), F32)

    kv_p, kv_s, win_p, win_s, ret_p, ret_s, conv_p, conv_s, mem_p = ([] for _ in range(9))
    for i in range(depth):
        mem_kv_p = _norm_matmul(mem2, mem_norm_g[i], w_mem_kv[i].astype(BF16), n_mem, PROJ_TN)
        mem_kv_p = mem_kv_p.reshape(bp, n_mem, 2 * MEM_W)
        mem_p.append(mem_kv_p.reshape(bp, n_mem, 2, MEM_HEADS, HEAD_DIM))
        mem_kv_s = cache_mem_kv[i].reshape(db, n_mem, 2 * MEM_W)
        if i % 2 == 0:
            a = i // 2
            w_tail = _prep_nsa_w(nsa_w_in[a])
            pe, w1, w2 = nsa_cmp_pe[a], nsa_cmp_w1[a], nsa_cmp_w2[a]
            n_main = NSA_MAIN_W // PROJ_TN
            proj_p = _norm_matmul(xp, norm1_g[i], nsa_w_in, PROJ_TM, PROJ_TN, a, n_main, w_tail)
            proj_s = _norm_matmul(xs, norm1_g[i], nsa_w_in, db * t_len, PROJ_TN, a, n_main, w_tail)
            p3 = proj_p.reshape(bp, s_len, NSA_PROJ_W)
            s3 = proj_s.reshape(db, t_len, NSA_PROJ_W)
            rows_lo, rows_hi = NSA_Q_W, NSA_Q_W + NSA_ROWS_W
            kv_p.append(p3[:, :, rows_lo:rows_hi].reshape(bp, s_len, 4, NSA_KV, HEAD_DIM))
            kv_s.append(s3[:, :, rows_lo:rows_hi].reshape(db, t_len, 4, NSA_KV, HEAD_DIM))
            cmp = _even_odd(_compress(proj_p, prompt_pages, pe, w1, w2, CMP_PAGES, col0=NSA_Q_W), bp,
                            planes_first=True)
            tok_p = _nsa_prompt(p3, cmp[0], cmp[1], NSA_COLS, NSA_TQ, NSA_TK)
            page_ids = page_table + a * n_phys
            cmp = _even_odd(_compress(cache, page_ids.reshape(-1), pe, w1, w2, CMP_PAGES), db)
            tok_s = _nsa_sample(s3, cache, page_ids, cmp[0], cmp[1], win_state, a * db, past, SAMPLE_PAGES)
            qm_block = NSA_MAIN_W // MEM_W
            win_p.append(p3[:, s_len - keep_p:, rows_hi:NSA_MAIN_W].reshape(bp, keep_p, 2, NSA_KV, HEAD_DIM))
            new_win = s3[:, :, rows_hi:NSA_MAIN_W].reshape(db, t_len, 2, NSA_KV, HEAD_DIM)
            win_s.append(jnp.concatenate([state_nsa_win[a], new_win], axis=1)[:, -wb:])
        else:
            bl = i // 2
            w_in = ret_w_in[bl].astype(BF16)
            proj_p = _norm_matmul(xp, norm1_g[i], w_in, PROJ_TM, PROJ_TN)
            proj_s = _norm_matmul(xs, norm1_g[i], w_in, db * t_len, PROJ_TN)
            p3 = proj_p.reshape(bp, s_len, -1)
            s3 = proj_s.reshape(db, t_len, -1)
            tok_p, sp = _retention(p3, ret_gn_g[bl], zero_state, *ret_tab_p)
            tok_s, ss = _retention(s3, ret_gn_g[bl], state_ret[bl], *ret_tab_s)
            ret_p.append(sp)
            ret_s.append(ss)
            qm_block = 4 * tok_w // MEM_W
        mem_out_p = _mem_attend(p3, qm_block, mem_kv_p, MEM_TQ)
        mem_out_s = _mem_attend(s3, qm_block, mem_kv_s, t_len)
        wo = w_o[i].astype(BF16)
        xp = _wo(xp, tok_p.reshape(bp * s_len, tok_w), mem_out_p.reshape(bp * s_len, MEM_W), wo[:tok_w], wo[tok_w:],
                 WO_TM, WO_TN)
        xs = _wo(xs, tok_s.reshape(db * t_len, tok_w), mem_out_s.reshape(db * t_len, MEM_W), wo[:tok_w], wo[tok_w:],
                 db * t_len, WO_TN)
        last = i == depth - 1
        ffn_w = (norm2_g[i], ffn_w_in[i].astype(BF16), ffn_conv_w[i], ffn_conv_b[i], ffn_w_out[i].astype(BF16))
        xp, tails = _ffn(xp, *ffn_w, zero_conv, final_norm_g, seq_len=s_len, tm=FFN_TM, tf=FFN_TF, tn=FFN_TN,
                         final_norm=last)
        conv_p.append(tails[s_len // FFN_TM - 1::s_len // FFN_TM])
        xs, up = _ffn(xs, *ffn_w, state_ffn_conv[i], final_norm_g, seq_len=t_len, tm=db * t_len, tf=FFN_TF,
                      tn=FFN_TN, final_norm=last)
        conv_s.append(up.reshape(db, t_len, fdim)[:, t_len - (CONV_W - 1):])
    return (xp.reshape(bp, s_len, d), xs.reshape(db, t_len, d), jnp.stack(kv_p), jnp.stack(kv_s), jnp.stack(win_p),
            jnp.stack(win_s), jnp.stack(ret_p), jnp.stack(ret_s), jnp.stack(conv_p), jnp.stack(conv_s),
            jnp.stack(mem_p))
```

```python
import functools

import jax
import jax.numpy as jnp
import numpy as np
from jax import lax
from jax.experimental import pallas as pl
from jax.experimental.pallas import tpu as pltpu

F32 = jnp.float32
BF16 = jnp.bfloat16

HEAD_DIM = 128
MEM_HEADS = 4
NSA_KV = 4
CMP_BLOCK = 32
SLC_BLOCK = 64
TOP_N = 16
WINDOW = 512
FORCE_BONUS = 1.0e4
RET_HEAD_DIM = 256
RET_CHUNK = 128
ROPE_BASE = 10000.0
CONV_W = 3
NORM_EPS = 1e-6
PAGE = 128

V7X_LANES = 128
V7X_VMEM_BYTES = 64 * 1024 * 1024
NEG_BIG = -1e30


def _cparams(n_axes, vmem_bytes):
    limit = min(int(vmem_bytes * 1.25) + (8 << 20), V7X_VMEM_BYTES - (4 << 20))
    return pltpu.CompilerParams(dimension_semantics=("arbitrary",) * n_axes, vmem_limit_bytes=limit)


def _dot(a, b):
    return jnp.dot(a, b, preferred_element_type=F32)


def _dot_nt(a, b):
    return lax.dot_general(a, b, (((1,), (1,)), ((), ())), preferred_element_type=F32)


def _rms(x, g):
    ms = jnp.mean(x * x, axis=-1, keepdims=True)
    return x * lax.rsqrt(ms + NORM_EPS) * g


def _norm_matmul_kernel(x_ref, g_ref, *refs, n_main):
    w_refs, (o_ref, hn_ref) = refs[:-2], refs[-2:]
    j = pl.program_id(1)

    @pl.when(j == 0)
    def _():
        hn_ref[...] = _rms(x_ref[...], g_ref[...]).astype(BF16)

    if len(w_refs) == 1:
        o_ref[...] = _dot(hn_ref[...], w_refs[0][...])
    else:
        @pl.when(j < n_main)
        def _():
            o_ref[...] = _dot(hn_ref[...], w_refs[0][...].astype(BF16))

        @pl.when(j >= n_main)
        def _():
            o_ref[...] = _dot(hn_ref[...], w_refs[1][...])


def _norm_matmul(x, g, w, tm, tn, layer=None, n_main=None, w_tail=None):
    m, k = x.shape
    if w_tail is None:
        n_main, n_tail, weights = w.shape[1] // tn, 0, [w]
        w_specs = [pl.BlockSpec((k, tn), lambda i, j: (0, j))]
    else:
        n_tail, weights = w_tail.shape[1] // tn, [w, w_tail]
        w_specs = [pl.BlockSpec((None, k, tn), lambda i, j: (layer, 0, jnp.minimum(j, n_main - 1))),
                   pl.BlockSpec((k, tn), lambda i, j: (0, jnp.maximum(j - n_main, 0)))]
    vmem = 2 * tm * k * 4 + tm * k * 2 + 2 * (len(weights) + 1) * k * tn * 2 + 2 * tm * tn * 4
    return pl.pallas_call(
        functools.partial(_norm_matmul_kernel, n_main=n_main),
        grid=(m // tm, n_main + n_tail),
        in_specs=[pl.BlockSpec((tm, k), lambda i, j: (i, 0)), pl.BlockSpec((1, k), lambda i, j: (0, 0))] + w_specs,
        out_specs=pl.BlockSpec((tm, tn), lambda i, j: (i, j)),
        out_shape=jax.ShapeDtypeStruct((m, (n_main + n_tail) * tn), F32),
        scratch_shapes=[pltpu.VMEM((tm, k), BF16)],
        compiler_params=_cparams(2, vmem),
        name="norm_matmul",
    )(x, g.reshape(1, k), *weights)


def _wo_kernel(x_ref, tok_ref, mem_ref, wt_ref, wm_ref, o_ref):
    o_ref[...] = (x_ref[...] + _dot(tok_ref[...].astype(BF16), wt_ref[...])
                  + _dot(mem_ref[...].astype(BF16), wm_ref[...]))


def _wo(x, tok, mem, w_tok, w_mem, tm, tn):
    m, d = x.shape
    kt, km = tok.shape[1], mem.shape[1]
    vmem = 2 * (tm * tn * 8 + tm * (kt + km) * 4 + (kt + km) * tn * 2)
    return pl.pallas_call(
        _wo_kernel,
        grid=(m // tm, d // tn),
        in_specs=[
            pl.BlockSpec((tm, tn), lambda i, j: (i, j)),
            pl.BlockSpec((tm, kt), lambda i, j: (i, 0)),
            pl.BlockSpec((tm, km), lambda i, j: (i, 0)),
            pl.BlockSpec((kt, tn), lambda i, j: (0, j)),
            pl.BlockSpec((km, tn), lambda i, j: (0, j)),
        ],
        out_specs=pl.BlockSpec((tm, tn), lambda i, j: (i, j)),
        out_shape=jax.ShapeDtypeStruct((m, d), F32),
        compiler_params=_cparams(2, vmem),
        name="wo_residual",
    )(x, tok, mem, w_tok, w_mem)


def _ffn_kernel(x_ref, g_ref, wa_ref, wg_ref, cw_ref, cb_ref, wo_ref, fg_ref, *refs,
                tiles_per_seq, seq_len, n_f, tn, final_norm):
    multi_seq = tiles_per_seq == 0
    prev_refs, (y_ref, tail_ref, hn_ref, act_ref, carry_ref) = refs[:-5], refs[-5:]
    i = pl.program_id(0)
    s = pl.program_id(1)
    tm, d = x_ref.shape

    @pl.when(s == 0)
    def _():
        hn_ref[...] = _rms(x_ref[...], g_ref[...]).astype(BF16)

    @pl.when(s < n_f)
    def _():
        hn = hn_ref[...]
        a = _dot(hn, wa_ref[...])
        gv = _dot(hn, wg_ref[...])
        row = lax.broadcasted_iota(jnp.int32, a.shape, 0)
        if multi_seq:
            t = row % seq_len
            a1 = jnp.where(t == 0, prev_refs[0][...], pltpu.roll(a, 1, axis=0))
            a2 = jnp.where(t < 2, prev_refs[1][...], pltpu.roll(a, 2, axis=0))
            tail_ref[...] = a
        else:
            prev = jnp.where((i % tiles_per_seq) == 0, prev_refs[0][0], carry_ref[s])
            a1 = jnp.where(row == 0, prev[1:2], pltpu.roll(a, 1, axis=0))
            a2 = jnp.where(row == 0, prev[0:1], jnp.where(row == 1, prev[1:2], pltpu.roll(a, 2, axis=0)))
            carry_ref[s] = a[tm - 2:tm]
            tail_ref[0] = a[tm - 2:tm]
        cw = cw_ref[...]
        ac = cb_ref[...] + a2 * cw[0:1] + a1 * cw[1:2] + a * cw[2:3]
        act_ref[s] = (ac * jax.nn.sigmoid(ac) * gv).astype(BF16)

    for n in range(d // tn):
        @pl.when(s == n_f + n)
        def _(n=n):
            act = jnp.concatenate([act_ref[k] for k in range(n_f)], axis=1)
            cols = slice(n * tn, (n + 1) * tn)
            y_ref[:, cols] = x_ref[:, cols] + _dot(act, wo_ref[...])

    if final_norm:
        @pl.when(s == n_f + d // tn - 1)
        def _():
            y_ref[...] = _rms(y_ref[...], fg_ref[...])


def _ffn(x, norm_g, w_in, conv_w, conv_b, w_out, buf, final_g, *, seq_len, tm, tf, tn, final_norm):
    m, d = x.shape
    fdim = w_out.shape[0]
    nf, nn = fdim // tf, d // tn
    multi_seq = tm > seq_len
    tiles_per_seq = 0 if multi_seq else seq_len // tm
    fa = lambda s: jnp.minimum(s, nf - 1)
    fb = lambda s: jnp.maximum(s - nf, 0)
    if multi_seq:
        n_seq = m // seq_len
        first = jnp.zeros((n_seq, seq_len, fdim), F32).at[:, 0].set(buf[:, 1])
        both = first.at[:, 0].set(buf[:, 0]).at[:, 1].set(buf[:, 1])
        prev = [first.reshape(m, fdim), both.reshape(m, fdim)]
        prev_specs = [pl.BlockSpec((tm, tf), lambda i, s: (i, fa(s)))] * 2
        tail_spec = pl.BlockSpec((tm, tf), lambda i, s: (i, fa(s)))
        tail_shape = jax.ShapeDtypeStruct((m, fdim), F32)
    else:
        prev = [buf]
        prev_specs = [pl.BlockSpec((1, CONV_W - 1, tf), lambda i, s: (i // tiles_per_seq, 0, fa(s)))]
        tail_spec = pl.BlockSpec((1, CONV_W - 1, tf), lambda i, s: (i, 0, fa(s)))
        tail_shape = jax.ShapeDtypeStruct((m // tm, CONV_W - 1, fdim), F32)
    vmem = (2 * tm * d * 4 + tm * d * 2 + 2 * tm * fdim * 2 + 2 * (2 * d * tf * 2 + fdim * tn * 2)
            + 10 * tm * tf * 4 + nf * 8 * tf * 4)
    kern = functools.partial(_ffn_kernel, tiles_per_seq=tiles_per_seq, seq_len=seq_len, n_f=nf, tn=tn,
                             final_norm=final_norm)
    return pl.pallas_call(
        kern,
        grid=(m // tm, nf + nn),
        in_specs=[
            pl.BlockSpec((tm, d), lambda i, s: (i, 0), pipeline_mode=pl.Buffered(1)),
            pl.BlockSpec((1, d), lambda i, s: (0, 0)),
            pl.BlockSpec((d, tf), lambda i, s: (0, fa(s))),
            pl.BlockSpec((d, tf), lambda i, s: (0, fa(s) + nf)),
            pl.BlockSpec((CONV_W, tf), lambda i, s: (0, fa(s))),
            pl.BlockSpec((1, tf), lambda i, s: (0, fa(s))),
            pl.BlockSpec((fdim, tn), lambda i, s: (0, fb(s))),
            pl.BlockSpec((1, d), lambda i, s: (0, 0)),
        ] + prev_specs,
        out_specs=[pl.BlockSpec((tm, d), lambda i, s: (i, 0), pipeline_mode=pl.Buffered(1)), tail_spec],
        out_shape=[jax.ShapeDtypeStruct((m, d), F32), tail_shape],
        scratch_shapes=[pltpu.VMEM((tm, d), BF16), pltpu.VMEM((nf, tm, tf), BF16),
                        pltpu.VMEM((nf, CONV_W - 1, tf), F32)],
        compiler_params=_cparams(2, vmem),
        name="conv_ffn",
    )(x, norm_g.reshape(1, d), w_in, w_in, conv_w, conv_b.reshape(1, fdim), w_out, final_g.reshape(1, d), *prev)


def _mem_kernel(q_ref, kv_ref, o_ref):
    q = q_ref[0] * (HEAD_DIM ** -0.5)
    kv = kv_ref[0]
    width = MEM_HEADS * HEAD_DIM
    outs = []
    for h in range(MEM_HEADS):
        lo = h * HEAD_DIM
        qh = q[:, lo:lo + HEAD_DIM].astype(BF16)
        kh = kv[:, lo:lo + HEAD_DIM].astype(BF16)
        vh = kv[:, width + lo:width + lo + HEAD_DIM].astype(BF16)
        s = _dot_nt(qh, kh)
        e = jnp.exp(s - jnp.max(s, axis=-1, keepdims=True))
        p = e / jnp.sum(e, axis=-1, keepdims=True)
        outs.append(_dot(p.astype(BF16), vh))
    o_ref[0] = jnp.concatenate(outs, axis=-1)


def _mem_attend(proj, qm_col_block, mem_kv, tq):
    b, t, _ = proj.shape
    width = MEM_HEADS * HEAD_DIM
    n_mem = mem_kv.shape[1]
    vmem = 2 * (2 * tq * width * 4 + n_mem * 2 * width * 4) + 8 * tq * n_mem * 4
    return pl.pallas_call(
        _mem_kernel,
        grid=(b, t // tq),
        in_specs=[
            pl.BlockSpec((1, tq, width), lambda bi, i: (bi, i, qm_col_block)),
            pl.BlockSpec((1, n_mem, 2 * width), lambda bi, i: (bi, 0, 0)),
        ],
        out_specs=pl.BlockSpec((1, tq, width), lambda bi, i: (bi, i, 0)),
        out_shape=jax.ShapeDtypeStruct((b, t, width), F32),
        compiler_params=_cparams(2, vmem),
        name="mem_attend",
    )(proj, mem_kv)


def _ret_kernel(q_ref, k_ref, v_ref, gate_ref, cos_ref, sin_ref, dmat_ref, qd_ref, kd_ref, cd_ref,
                gn_ref, s0_ref, tok_ref, s_out_ref, s_ref, *, n_chunks):
    c = pl.program_id(1)
    n_heads, dk = s_ref.shape[0], s_ref.shape[1]

    @pl.when(c == 0)
    def _():
        s_ref[...] = s0_ref[0]

    half = dk // 2
    cos = cos_ref[...]
    sin = sin_ref[...]
    rows = q_ref.shape[1]
    c_len = cos.shape[0]

    def load(ref, h):
        x = ref[0, :, h * dk:(h + 1) * dk]
        return x if rows == c_len else jnp.concatenate([x, jnp.zeros((c_len - rows, dk), F32)], axis=0)

    def rot(x):
        x1, x2 = x[:, :half], x[:, half:]
        return jnp.concatenate([x1 * cos - x2 * sin, x1 * sin + x2 * cos], axis=-1)

    for h in range(n_heads):
        q = rot(load(q_ref, h))
        k = rot(load(k_ref, h)) * (dk ** -0.5)
        v = load(v_ref, h).astype(BF16)
        qb = q.astype(BF16)
        s_old = s_ref[h]
        inner = _dot_nt(qb, k.astype(BF16)) * dmat_ref[h]
        o = _dot(inner.astype(BF16), v) + _dot(qb, s_old.astype(BF16)) * qd_ref[h]
        kd = (k * kd_ref[h]).astype(BF16)
        s_ref[h] = s_old * cd_ref[h] + _dot(kd.T, v)

        mu = jnp.mean(o, axis=-1, keepdims=True)
        dev = o - mu
        var = jnp.mean(dev * dev, axis=-1, keepdims=True)
        y = dev * lax.rsqrt(var + NORM_EPS) * gn_ref[h]
        gate = gate_ref[0, :, h * dk:(h + 1) * dk]
        tok_ref[0, :, h * dk:(h + 1) * dk] = gate * jax.nn.sigmoid(gate) * y[:rows]

    @pl.when(c == n_chunks - 1)
    def _():
        s_out_ref[0] = s_ref[...]


def _retention(proj, gn_g, s0, cos, sin, dmat, q_decay, k_decay, c_decay):
    b, t, _ = proj.shape
    h = s0.shape[1]
    dk = RET_HEAD_DIM
    c = dmat.shape[1]
    rows = min(t, c)
    n = t // rows
    vmem = 2 * (5 * rows * h * dk * 4 + h * c * c * 4 + 2 * h * c * 128 * 4 + 2 * h * dk * dk * 4) + 3 * h * dk * dk * 4
    blk = lambda group: pl.BlockSpec((1, rows, h * dk), lambda bi, ci: (bi, ci, group))
    whole = lambda shape: pl.BlockSpec(shape, lambda bi, ci: (0,) * len(shape))
    return pl.pallas_call(
        functools.partial(_ret_kernel, n_chunks=n),
        grid=(b, n),
        in_specs=[
            blk(0), blk(1), blk(2), blk(3),
            pl.BlockSpec((c, dk // 2), lambda bi, ci: (ci, 0)),
            pl.BlockSpec((c, dk // 2), lambda bi, ci: (ci, 0)),
            whole((h, c, c)), whole((h, c, 1)), whole((h, c, 1)), whole((h, 1, dk)), whole((h, 1, dk)),
            pl.BlockSpec((1, h, dk, dk), lambda bi, ci: (bi, 0, 0, 0)),
        ],
        out_specs=[
            pl.BlockSpec((1, rows, h * dk), lambda bi, ci: (bi, ci, 0)),
            pl.BlockSpec((1, h, dk, dk), lambda bi, ci: (bi, 0, 0, 0)),
        ],
        out_shape=[jax.ShapeDtypeStruct((b, t, h * dk), F32),
                   jax.ShapeDtypeStruct((b, h, dk, dk), F32)],
        scratch_shapes=[pltpu.VMEM((h, dk, dk), F32)],
        compiler_params=_cparams(2, vmem),
        name="retention",
    )(proj, proj, proj, proj, cos, sin, dmat, q_decay, k_decay, c_decay, gn_g.reshape(h, 1, dk), s0)


def _retention_tables(n_heads, c_pad, c_real, pos):
    log_g = jnp.log1p(-jnp.exp2(-5.0 - jnp.arange(n_heads, dtype=F32)))
    i = jnp.arange(c_pad, dtype=F32)
    live = i < c_real
    diff = i[:, None] - i[None, :]
    dmat = jnp.where((diff >= 0) & live[:, None] & live[None, :],
                     jnp.exp(jnp.maximum(diff, 0.0)[None] * log_g[:, None, None]), 0.0)
    q_decay = jnp.exp((i + 1.0)[None, :] * log_g[:, None])[..., None]
    k_decay = jnp.where(live[None, :], jnp.exp((c_real - 1.0 - i)[None, :] * log_g[:, None]), 0.0)[..., None]
    c_decay = jnp.broadcast_to(jnp.exp(c_real * log_g)[:, None, None], (n_heads, 1, RET_HEAD_DIM))
    half = RET_HEAD_DIM // 2
    inv = ROPE_BASE ** (-jnp.arange(half, dtype=F32) / half)
    ang = pos.astype(F32)[:, None] * inv[None, :]
    return jnp.cos(ang), jnp.sin(ang), dmat, q_decay, k_decay, c_decay


KV_PLANES = 4 * NSA_KV
CMP_PLANES = 2 * NSA_KV


def _compress_kernel(plist_ref, src_ref, pe_ref, w1_ref, w2_ref, o_ref, buf_ref, lhs_ref, sem_ref, *,
                     pages, n_steps, col0):
    s = pl.program_id(0)
    slot = s % 2
    bpp = PAGE // CMP_BLOCK
    nblk = pages * bpp
    rows = nblk * CMP_PLANES

    def copies(step, to_slot):
        cps = []
        for p in range(pages):
            tok0 = plist_ref[step * pages + p] * PAGE
            for n in range(bpp):
                toks = pl.ds(tok0 + n * CMP_BLOCK, CMP_BLOCK)
                if col0 is None:
                    src = src_ref.at[toks, pl.ds(0, CMP_PLANES), :]
                else:
                    src = src_ref.at[toks, pl.ds(col0, CMP_PLANES * HEAD_DIM)]
                cps.append(pltpu.make_async_copy(src, buf_ref.at[to_slot, :, p * bpp + n], sem_ref.at[to_slot]))
        return cps

    @pl.when(s == 0)
    def _():
        for cp in copies(0, 0):
            cp.start()

    @pl.when(s + 1 < n_steps)
    def _():
        for cp in copies(s + 1, 1 - slot):
            cp.start()

    for cp in copies(s, slot):
        cp.wait()

    row = lax.broadcasted_iota(jnp.int32, (rows, HEAD_DIM), 0)
    if col0 is None:
        for c in range(CMP_BLOCK):
            x = buf_ref[slot, c] + pe_ref[c]
            lhs_ref[:, c * HEAD_DIM:(c + 1) * HEAD_DIM] = x.reshape(rows, HEAD_DIM).astype(BF16)
        is_key = (row % CMP_PLANES) < NSA_KV
    else:
        for c in range(CMP_BLOCK):
            x = buf_ref[slot, c]
            for j in range(CMP_PLANES):
                xj = x[:, j * HEAD_DIM:(j + 1) * HEAD_DIM] + pe_ref[c, j:j + 1, :]
                lhs_ref[j * nblk:(j + 1) * nblk, c * HEAD_DIM:(c + 1) * HEAD_DIM] = xj.astype(BF16)
        is_key = row < NSA_KV * nblk
    h = _dot(lhs_ref[...], w1_ref[...])
    h = jax.nn.gelu(jnp.where(is_key, h[:, :HEAD_DIM], h[:, HEAD_DIM:]))
    out = _dot(h.astype(BF16), w2_ref[...])
    o_ref[...] = jnp.where(is_key, out[:, :HEAD_DIM], out[:, HEAD_DIM:]).reshape(o_ref.shape)


def _compress(src, plist, pe, w1, w2, pages, col0=None):
    n_pages = plist.shape[0]
    nblk = pages * (PAGE // CMP_BLOCK)
    n_total = n_pages * (PAGE // CMP_BLOCK)
    rows = nblk * CMP_PLANES
    k_dim = CMP_BLOCK * HEAD_DIM
    pe8 = jnp.repeat(pe.transpose(1, 0, 2), NSA_KV, axis=1)
    w1b = jnp.concatenate([w1[0], w1[1]], axis=1).astype(BF16)
    w2b = jnp.concatenate([w2[0], w2[1]], axis=1).astype(BF16)
    if col0 is None:
        buf_shape = (2, CMP_BLOCK, nblk, CMP_PLANES, HEAD_DIM)
        out_spec = pl.BlockSpec((nblk, CMP_PLANES, HEAD_DIM), lambda s, pr: (s, 0, 0))
        out_shape = (n_total, CMP_PLANES, HEAD_DIM)
    else:
        buf_shape = (2, CMP_BLOCK, nblk, CMP_PLANES * HEAD_DIM)
        out_spec = pl.BlockSpec((CMP_PLANES, nblk, HEAD_DIM), lambda s, pr: (0, s, 0))
        out_shape = (CMP_PLANES, n_total, HEAD_DIM)
    vmem = 2 * CMP_BLOCK * rows * HEAD_DIM * 4 + rows * k_dim * 2 + 2 * k_dim * 2 * HEAD_DIM * 2 + 8 * rows * HEAD_DIM * 4
    grid_spec = pltpu.PrefetchScalarGridSpec(
        num_scalar_prefetch=1,
        grid=(n_pages // pages,),
        in_specs=[
            pl.BlockSpec(memory_space=pl.ANY),
            pl.BlockSpec((CMP_BLOCK, CMP_PLANES, HEAD_DIM), lambda s, pr: (0, 0, 0)),
            pl.BlockSpec((k_dim, 2 * HEAD_DIM), lambda s, pr: (0, 0)),
            pl.BlockSpec((HEAD_DIM, 2 * HEAD_DIM), lambda s, pr: (0, 0)),
        ],
        out_specs=out_spec,
        scratch_shapes=[pltpu.VMEM(buf_shape, F32), pltpu.VMEM((rows, k_dim), BF16), pltpu.SemaphoreType.DMA((2,))],
    )
    return pl.pallas_call(
        functools.partial(_compress_kernel, pages=pages, n_steps=n_pages // pages, col0=col0),
        grid_spec=grid_spec,
        out_shape=jax.ShapeDtypeStruct(out_shape, F32),
        compiler_params=_cparams(1, vmem),
        name="nsa_compress",
    )(plist, src, pe8, w1b, w2b)


def _even_odd(c, n_seq, planes_first=False):
    d = c.shape[-1]
    nc = c.shape[1 if planes_first else 0] // n_seq
    if planes_first:
        c = c.reshape(2, NSA_KV, n_seq, nc // 2, 2, d).transpose(0, 1, 2, 4, 3, 5)
    else:
        c = c.reshape(n_seq, nc // 2, 2, 2, NSA_KV, d).transpose(3, 4, 0, 2, 1, 5)
    return c.reshape(2, NSA_KV, n_seq, nc, d).astype(BF16)


def _cmp_attend(qb, kc, vc, qpos_col):
    nc = kc.shape[0]
    lane = lax.broadcasted_iota(jnp.int32, (1, nc), 1)
    blk = jnp.where(lane < nc // 2, 2 * lane, 2 * (lane - nc // 2) + 1)
    vis = (blk * CMP_BLOCK + (CMP_BLOCK - 1)) <= qpos_col
    s = jnp.where(vis, _dot_nt(qb, kc), -jnp.inf)
    m = jnp.max(s, axis=-1, keepdims=True)
    m = jnp.where(m > -jnp.inf, m, 0.0)
    e = jnp.where(vis, jnp.exp(s - m), 0.0)
    p = e / jnp.maximum(jnp.sum(e, axis=-1, keepdims=True), 1e-30)
    return _dot(p.astype(BF16), vc), p


def _select_blocks(p_t, qpos_row, n_sel, m_hi, score_ref):
    nb, nq = p_t.shape
    blk = lax.broadcasted_iota(jnp.int32, (nb, nq), 0)
    valid = blk * SLC_BLOCK <= qpos_row
    cur = qpos_row // SLC_BLOCK
    forced = (blk == 0) | (blk == cur) | (blk == cur - 1)
    score = jnp.where(valid, p_t + jnp.where(forced, FORCE_BONUS, 0.0), -jnp.inf)
    score_ref[...] = score
    rows = min(64, nb)
    outs = []
    for r0 in range(0, nb, rows):
        sc = score[r0:r0 + rows]
        bk = blk[r0:r0 + rows]

        def body(mi, rank, sc=sc, bk=bk):
            other = score_ref[pl.ds(mi, 1), :]
            ge = jnp.where(other >= sc, 1.0, 0.0)
            gt = jnp.where(other > sc, 1.0, 0.0)
            return rank + jnp.where(bk > mi, ge, gt)

        rank = lax.fori_loop(0, m_hi, body, jnp.zeros_like(sc))
        outs.append(jnp.where((rank < n_sel) & valid[r0:r0 + rows], 1.0, 0.0))
    return jnp.concatenate(outs, axis=0) if len(outs) > 1 else outs[0]


def _lane_tile(x, n_lanes):
    reps = n_lanes // x.shape[1]
    return x if reps == 1 else jnp.concatenate([x] * reps, axis=1)


def _softmax_step(s, vis, v, m_ref, l_ref, acc_ref, r0, value_shift=0, bias=None):
    rows, n_keys = s.shape
    sl = pl.ds(r0, rows)
    s = jnp.where(vis, s, NEG_BIG) if bias is None else s + bias
    m_old = m_ref[sl, :]
    m_new = jnp.maximum(m_old, jnp.max(s, axis=-1, keepdims=True))
    alpha = jnp.exp(m_old - m_new)
    p = jnp.exp(s - _lane_tile(m_new, n_keys))
    l_ref[sl, :] = alpha * l_ref[sl, :] + jnp.sum(p, axis=-1, keepdims=True)
    pv = pltpu.roll(p, value_shift, axis=1) if value_shift else p
    acc_ref[sl, :] = alpha * acc_ref[sl, :] + _dot(pv.astype(BF16), v)
    m_ref[sl, :] = m_new


def _softmax_finish(m_ref, l_ref, acc_ref):
    seen = m_ref[...] > 0.5 * NEG_BIG
    return jnp.where(seen, acc_ref[...] / jnp.maximum(l_ref[...], 1e-30), 0.0)


def _softmax_once(s, vis, v):
    groups = s.shape[1] // V7X_LANES
    s = jnp.where(vis, s, NEG_BIG)
    part = lambda x, c: x[:, c * V7X_LANES:(c + 1) * V7X_LANES]
    m = functools.reduce(jnp.maximum, [part(s, c) for c in range(groups)])
    m = jnp.max(m, axis=-1, keepdims=True)
    e = jnp.exp(s - m)
    l = functools.reduce(jnp.add, [part(e, c) for c in range(groups)])
    l = jnp.maximum(jnp.sum(l, axis=-1, keepdims=True), 1e-30)
    return jnp.where(m > 0.5 * NEG_BIG, _dot(e.astype(BF16), v) / l, 0.0)


def _block_expander(n_blocks, key):
    blk = lax.broadcasted_iota(jnp.int32, (n_blocks, key.shape[1]), 0)
    return jnp.where(blk == key // SLC_BLOCK, 1.0, 0.0).astype(BF16)


def _nsa_prompt_kernel(q_ref, kc_ref, vc_ref, ks_ref, vs_ref, kw_ref, vw_ref, gate_ref, tok_ref,
                       score_ref, sel_ref, s_ref, m_ref, l_ref, acc_ref, *, tk):
    i = pl.program_id(2)
    tq = q_ref.shape[1]
    r = q_ref.shape[2] // HEAD_DIM
    nb = kc_ref.shape[2] // (SLC_BLOCK // CMP_BLOCK)
    nbp = -(-nb // V7X_LANES) * V7X_LANES
    t0 = i * tq
    q = q_ref[0] * (HEAD_DIM ** -0.5)
    qb = jnp.concatenate([q[:, h * HEAD_DIM:(h + 1) * HEAD_DIM] for h in range(r)], axis=0).astype(BF16)
    tcol = t0 + lax.broadcasted_iota(jnp.int32, (tq, 1), 0)
    trow = t0 + lax.broadcasted_iota(jnp.int32, (1, tq), 1)

    o_cmp, p = _cmp_attend(qb, kc_ref[0, 0], vc_ref[0, 0], jnp.concatenate([tcol] * r, axis=0))
    p_grp = p[0:tq]
    for h in range(1, r):
        p_grp = p_grp + p[h * tq:(h + 1) * tq]
    p_pair = p_grp + pltpu.roll(p_grp, nb, axis=1)
    m_hi = jnp.minimum((t0 + tq - 1) // SLC_BLOCK + 1, nb)
    sel_t = _select_blocks(p_pair.T[:nb], trow, TOP_N, m_hi, score_ref)
    if nbp > nb:
        sel_t = jnp.concatenate([sel_t, jnp.zeros((nbp - nb, tq), F32)], axis=0)
    sel = sel_t.T.astype(BF16)
    bpt = tk // SLC_BLOCK
    for jj in range(sel_ref.shape[0]):
        sel_ref[jj] = sel[:, jj * bpt:(jj + 1) * bpt]
    local_key = lax.broadcasted_iota(jnp.int32, (1, tk), 1)

    n_kt = (t0 + tq + tk - 1) // tk
    lane_groups = tk // V7X_LANES
    m_ref[...] = jnp.full(m_ref.shape, NEG_BIG, F32)

    def score_tile(j, carry):
        k0 = pl.multiple_of(j * tk, tk)
        s = _dot_nt(qb, ks_ref[0, pl.ds(k0, tk), :].astype(BF16))
        picked = _dot(sel_ref[j], _block_expander(bpt, local_key))
        kpos = k0 + lax.broadcasted_iota(jnp.int32, (tq, tk), 1)
        vis = (picked > 0.5) & (kpos <= tcol)
        for h in range(r):
            rows = pl.ds(h * tq, tq)
            sh = jnp.where(vis, s[h * tq:(h + 1) * tq], NEG_BIG)
            s_ref[j, rows, :] = sh
            m = m_ref[rows, :]
            for c in range(lane_groups):
                m = jnp.maximum(m, sh[:, c * V7X_LANES:(c + 1) * V7X_LANES])
            m_ref[rows, :] = m
        return carry

    lax.fori_loop(0, n_kt, score_tile, 0)
    m_row = jnp.max(m_ref[...], axis=-1, keepdims=True)
    m_ref[...] = jnp.broadcast_to(m_row, m_ref.shape)
    l_ref[...] = jnp.zeros(l_ref.shape, F32)
    acc_ref[...] = jnp.zeros(acc_ref.shape, F32)

    def value_tile(j, carry):
        k0 = pl.multiple_of(j * tk, tk)
        p = jnp.exp(s_ref[j] - _lane_tile(m_ref[...], tk))
        l = l_ref[...]
        for c in range(lane_groups):
            l = l + p[:, c * V7X_LANES:(c + 1) * V7X_LANES]
        l_ref[...] = l
        acc_ref[...] += _dot(p.astype(BF16), vs_ref[0, pl.ds(k0, tk), :].astype(BF16))
        return carry

    lax.fori_loop(0, n_kt, value_tile, 0)
    denom = jnp.maximum(jnp.sum(l_ref[...], axis=-1, keepdims=True), 1e-30)
    o_slc = jnp.where(m_row > 0.5 * NEG_BIG, acc_ref[...] / denom, 0.0)

    span = WINDOW + tq
    w0 = pl.multiple_of(jnp.maximum(t0 + tq - span, 0), tq)
    kw = kw_ref[0, pl.ds(w0, span), :].astype(BF16)
    vw = vw_ref[0, pl.ds(w0, span), :].astype(BF16)
    s = _dot_nt(qb, kw)
    kpos = w0 + lax.broadcasted_iota(jnp.int32, (tq, span), 1)
    vis = (kpos <= tcol) & (kpos > tcol - WINDOW)
    o_win = [_softmax_once(s[h * tq:(h + 1) * tq], vis, vw) for h in range(r)]

    gt = jax.nn.sigmoid(gate_ref[0])
    outs = []
    for h in range(r):
        rows = slice(h * tq, (h + 1) * tq)
        outs.append(gt[:, 3 * h:3 * h + 1] * o_cmp[rows] + gt[:, 3 * h + 1:3 * h + 2] * o_slc[rows]
                    + gt[:, 3 * h + 2:3 * h + 3] * o_win[h])
    tok_ref[0] = jnp.concatenate(outs, axis=-1)


def _nsa_prompt(proj, kc, vc, cols, tq, tk):
    b, t, _ = proj.shape
    g = kc.shape[0]
    nc = kc.shape[2]
    r = 3
    nb = nc // (SLC_BLOCK // CMP_BLOCK)
    assert t >= WINDOW + tq and t % tk == 0 and tk % tq == 0 and nb % 64 == 0 and nc <= V7X_LANES
    kv_spec = lambda c0: pl.BlockSpec((1, t, HEAD_DIM), lambda bi, gi, i: (bi, 0, c0 + gi))
    cmp_spec = pl.BlockSpec((1, 1, nc, HEAD_DIM), lambda bi, gi, i: (gi, bi, 0, 0))
    vmem = 2 * 4 * t * HEAD_DIM * 4 + r * tq * t * 4 + 24 * r * tq * max(tk, WINDOW + tq) * 4
    return pl.pallas_call(
        functools.partial(_nsa_prompt_kernel, tk=tk),
        grid=(b, g, t // tq),
        in_specs=[
            pl.BlockSpec((1, tq, r * HEAD_DIM), lambda bi, gi, i: (bi, i, cols["q"] // r + gi)),
            cmp_spec, cmp_spec,
            kv_spec(cols["ks"]), kv_spec(cols["vs"]), kv_spec(cols["kw"]), kv_spec(cols["vw"]),
            pl.BlockSpec((1, tq, HEAD_DIM), lambda bi, gi, i: (bi, i, cols["gates"] + gi)),
        ],
        out_specs=pl.BlockSpec((1, tq, r * HEAD_DIM), lambda bi, gi, i: (bi, i, gi)),
        out_shape=jax.ShapeDtypeStruct((b, t, g * r * HEAD_DIM), F32),
        scratch_shapes=[pltpu.VMEM((nb, tq), F32), pltpu.VMEM((t // tk, tq, tk // SLC_BLOCK), BF16),
                        pltpu.VMEM((t // tk, r * tq, tk), F32),
                        pltpu.VMEM((r * tq, V7X_LANES), F32), pltpu.VMEM((r * tq, V7X_LANES), F32),
                        pltpu.VMEM((r * tq, HEAD_DIM), F32)],
        compiler_params=_cparams(3, vmem),
        name="nsa_prompt",
    )(proj, kc, vc, proj, proj, proj, proj, proj)


NSA_HEADS = 12
NSA_KVW = NSA_KV * HEAD_DIM
NSA_Q_W = NSA_HEADS * HEAD_DIM
NSA_ROWS_W = 4 * NSA_KVW
NSA_WIN_W = 2 * NSA_KVW
MEM_W = MEM_HEADS * HEAD_DIM
NSA_GATES = 3 * NSA_HEADS
NSA_MAIN_W = NSA_Q_W + NSA_ROWS_W + NSA_WIN_W
NSA_PROJ_W = NSA_MAIN_W + MEM_W + NSA_KV * V7X_LANES
NSA_COLS = {
    "q": 0,
    "ks": (NSA_Q_W + 2 * NSA_KVW) // V7X_LANES,
    "vs": (NSA_Q_W + 3 * NSA_KVW) // V7X_LANES,
    "kw": (NSA_Q_W + NSA_ROWS_W) // V7X_LANES,
    "vw": (NSA_Q_W + NSA_ROWS_W + NSA_KVW) // V7X_LANES,
    "gates": (NSA_MAIN_W + MEM_W) // V7X_LANES,
}


def _prep_nsa_w(w):
    d = w.shape[0]
    gates = w[:, NSA_MAIN_W:NSA_MAIN_W + NSA_GATES].reshape(d, NSA_KV, NSA_GATES // NSA_KV)
    gates = jnp.pad(gates, ((0, 0), (0, 0), (0, V7X_LANES - NSA_GATES // NSA_KV))).reshape(d, NSA_KV * V7X_LANES)
    return jnp.concatenate([w[:, NSA_MAIN_W + NSA_GATES:], gates], axis=1).astype(BF16)


def _stack_heads(q, n_heads):
    return jnp.concatenate([q[:, h * HEAD_DIM:(h + 1) * HEAD_DIM] for h in range(n_heads)], axis=0)


def _nsa_sample_cmp_kernel(q_ref, kc_ref, vc_ref, ocmp_ref, pslc_ref, *, past):
    t = q_ref.shape[1]
    g_n = kc_ref.shape[0]
    r = q_ref.shape[2] // HEAD_DIM // g_n
    nb = kc_ref.shape[2] // (SLC_BLOCK // CMP_BLOCK)
    q = q_ref[0] * (HEAD_DIM ** -0.5)
    qpos = past + lax.broadcasted_iota(jnp.int32, (t, 1), 0)
    qpos_r = jnp.concatenate([qpos] * r, axis=0)
    for g in range(g_n):
        qb = _stack_heads(q[:, g * r * HEAD_DIM:(g + 1) * r * HEAD_DIM], r).astype(BF16)
        o, p = _cmp_attend(qb, kc_ref[g, 0], vc_ref[g, 0], qpos_r)
        ocmp_ref[0, g * r * t:(g + 1) * r * t, :] = o
        p_grp = p[0:t]
        for h in range(1, r):
            p_grp = p_grp + p[h * t:(h + 1) * t]
        p_pair = p_grp + pltpu.roll(p_grp, nb, axis=1)
        pslc_ref[0, g * t:(g + 1) * t, :] = p_pair[:, :nb]


def _nsa_sample_cmp(proj, kc, vc, past):
    b, t, _ = proj.shape
    g, _, nc, _ = kc.shape
    nb = nc // (SLC_BLOCK // CMP_BLOCK)
    cmp_spec = pl.BlockSpec((g, 1, nc, HEAD_DIM), lambda bi: (0, bi, 0, 0))
    vmem = 2 * (t * NSA_Q_W * 4 + 2 * g * nc * HEAD_DIM * 2) + 64 * nc * 4 * 8
    return pl.pallas_call(
        functools.partial(_nsa_sample_cmp_kernel, past=past),
        grid=(b,),
        in_specs=[pl.BlockSpec((1, t, NSA_Q_W), lambda bi: (bi, 0, 0)), cmp_spec, cmp_spec],
        out_specs=[pl.BlockSpec((1, NSA_HEADS * t, HEAD_DIM), lambda bi: (bi, 0, 0)),
                   pl.BlockSpec((1, g * t, nb), lambda bi: (bi, 0, 0))],
        out_shape=[jax.ShapeDtypeStruct((b, NSA_HEADS * t, HEAD_DIM), F32),
                   jax.ShapeDtypeStruct((b, g * t, nb), F32)],
        compiler_params=_cparams(1, vmem),
        name="nsa_sample_cmp",
    )(proj, kc, vc)


def _rank_kernel(p_ref, qpos_ref, sel_ref, score_ref, *, n_sel):
    sel_ref[...] = _select_blocks(p_ref[...], qpos_ref[...], n_sel, p_ref.shape[0], score_ref)


def _rank_blocks(p_t, qpos, n_sel):
    nb, nq = p_t.shape
    tq = V7X_LANES
    return pl.pallas_call(
        functools.partial(_rank_kernel, n_sel=n_sel),
        grid=(nq // tq,),
        in_specs=[pl.BlockSpec((nb, tq), lambda i: (0, i)), pl.BlockSpec((1, tq), lambda i: (0, i))],
        out_specs=pl.BlockSpec((nb, tq), lambda i: (0, i)),
        out_shape=jax.ShapeDtypeStruct((nb, nq), F32),
        scratch_shapes=[pltpu.VMEM((nb, tq), F32)],
        compiler_params=_cparams(1, 8 * nb * tq * 4),
        name="nsa_rank",
    )(p_t, qpos)


def _nsa_sample_attend_kernel(pages_ref, *refs, past, n_steps, pps):
    kv_refs = refs[:pps]
    (q_ref, sel_ref, ocmp_ref, ksn_ref, vsn_ref, kwn_ref, vwn_ref, wst_ref, gate_ref, tok_ref,
     qs_ref, own_ref, m_ref, l_ref, acc_ref) = refs[pps:]
    j = pl.program_id(1)
    t = q_ref.shape[1]
    g_n = NSA_KV
    r = NSA_HEADS // g_n
    rg = r * t
    span = pps * PAGE
    half = KV_PLANES // 2
    cols = span * half

    @pl.when(j == 0)
    def _():
        qs_ref[...] = _stack_heads(q_ref[0] * (HEAD_DIM ** -0.5), NSA_HEADS)
        m_ref[...] = jnp.full(m_ref.shape, NEG_BIG, F32)
        l_ref[...] = jnp.zeros(l_ref.shape, F32)
        acc_ref[...] = jnp.zeros(acc_ref.shape, F32)
        lane = lax.broadcasted_iota(jnp.int32, (g_n * t, cols), 1)
        own = lax.broadcasted_iota(jnp.int32, (g_n * t, cols), 0) // t
        own_ref[...] = jnp.where((lane % half) == own, 1.0, 0.0)

    kvb = jnp.concatenate([kv[...].reshape(PAGE * half, HEAD_DIM) for kv in kv_refs], axis=0).astype(BF16)
    s = _dot_nt(qs_ref[...].astype(BF16), kvb)
    local_key = lax.broadcasted_iota(jnp.int32, (1, cols), 1) // half
    picked = _dot(sel_ref[0, 0], _block_expander(span // SLC_BLOCK, local_key))
    bias = (picked * own_ref[...] - 1.0) * (-NEG_BIG)
    bias = jnp.concatenate([bias[g * t:(g + 1) * t] for g in range(g_n) for _ in range(r)], axis=0)
    _softmax_step(s, None, kvb, m_ref, l_ref, acc_ref, 0, value_shift=g_n, bias=bias)

    @pl.when(j == n_steps - 1)
    def _():
        tcol = jnp.concatenate([lax.broadcasted_iota(jnp.int32, (t, 1), 0)] * r, axis=0)
        pad = jnp.zeros((PAGE - t, HEAD_DIM), F32)

        def new_keys(ref, g):
            return jnp.concatenate([ref[0][:, g * HEAD_DIM:(g + 1) * HEAD_DIM], pad], axis=0)

        lane = lax.broadcasted_iota(jnp.int32, (rg, PAGE), 1)
        for g in range(g_n):
            qg = qs_ref[g * rg:(g + 1) * rg, :].astype(BF16)
            _softmax_step(_dot_nt(qg, new_keys(ksn_ref, g).astype(BF16)), lane <= tcol,
                          new_keys(vsn_ref, g).astype(BF16), m_ref, l_ref, acc_ref, g * rg)
        o_slc = _softmax_finish(m_ref, l_ref, acc_ref)

        wb = wst_ref.shape[1]
        kpos = past - wb + lax.broadcasted_iota(jnp.int32, (rg, wb + PAGE), 1)
        qpos = past + tcol
        vis = (kpos <= qpos) & (kpos > qpos - WINDOW)
        wst = wst_ref[0]
        gates = gate_ref[0]
        ocmp = ocmp_ref[0]
        outs = []
        for g in range(g_n):
            qg = qs_ref[g * rg:(g + 1) * rg, :].astype(BF16)
            kw = jnp.concatenate([wst[:, g * HEAD_DIM:(g + 1) * HEAD_DIM], new_keys(kwn_ref, g)], axis=0)
            vw = jnp.concatenate([wst[:, NSA_KVW + g * HEAD_DIM:NSA_KVW + (g + 1) * HEAD_DIM],
                                  new_keys(vwn_ref, g)], axis=0)
            o_win = _softmax_once(_dot_nt(qg, kw.astype(BF16)), vis, vw.astype(BF16))
            gt = jax.nn.sigmoid(gates[:, g * V7X_LANES:(g + 1) * V7X_LANES])
            for h in range(r):
                rows = slice(g * rg + h * t, g * rg + (h + 1) * t)
                outs.append(gt[:, 3 * h:3 * h + 1] * ocmp[rows] + gt[:, 3 * h + 1:3 * h + 2] * o_slc[rows]
                            + gt[:, 3 * h + 2:3 * h + 3] * o_win[h * t:(h + 1) * t])
        tok_ref[0] = jnp.concatenate(outs, axis=-1)


def _nsa_sample_attend(cache, pages, proj, sel, ocmp, win_state, win_index0, past, pps):
    b, t, _ = proj.shape
    npg = pages.shape[0] // b
    nb = sel.shape[2]
    wb = win_state.shape[1]
    n_steps = npg // pps
    assert nb * SLC_BLOCK == npg * PAGE == past and t <= SLC_BLOCK and npg % pps == 0
    col = lambda first: first * V7X_LANES // NSA_KVW
    new_spec = lambda name: pl.BlockSpec((1, t, NSA_KVW), lambda bi, j, pg: (bi, 0, col(NSA_COLS[name])))
    page_spec = lambda k: pl.BlockSpec((PAGE, KV_PLANES // 2, HEAD_DIM),
                                       lambda bi, j, pg: (pg[bi * npg + j * pps + k], 1, 0))
    rows = NSA_HEADS * t
    bps = pps * PAGE // SLC_BLOCK
    sel = sel.reshape(b, NSA_KV * t, n_steps, bps).transpose(0, 2, 1, 3)
    cols = pps * PAGE * (KV_PLANES // 2)
    vmem = (2 * (pps * PAGE * NSA_KVW * 2 * 4 + wb * 2 * NSA_KVW * 4 + t * NSA_PROJ_W * 4)
            + 6 * pps * PAGE * HEAD_DIM * 4 + 8 * (wb + PAGE) * 128 * 4 + 8 * NSA_KV * t * cols * 4
            + 6 * rows * cols * 4)
    grid_spec = pltpu.PrefetchScalarGridSpec(
        num_scalar_prefetch=1,
        grid=(b, n_steps),
        in_specs=[page_spec(k) for k in range(pps)] + [
            pl.BlockSpec((1, t, NSA_Q_W), lambda bi, j, pg: (bi, 0, 0)),
            pl.BlockSpec((1, 1, NSA_KV * t, bps), lambda bi, j, pg: (bi, j, 0, 0)),
            pl.BlockSpec((1, rows, HEAD_DIM), lambda bi, j, pg: (bi, 0, 0)),
            new_spec("ks"), new_spec("vs"), new_spec("kw"), new_spec("vw"),
            pl.BlockSpec((1, wb, 2 * NSA_KVW), lambda bi, j, pg: (win_index0 + bi, 0, 0)),
            pl.BlockSpec((1, t, NSA_KV * V7X_LANES), lambda bi, j, pg: (bi, 0, col(NSA_COLS["gates"]))),
        ],
        out_specs=pl.BlockSpec((1, t, NSA_Q_W), lambda bi, j, pg: (bi, 0, 0)),
        scratch_shapes=[pltpu.VMEM((rows, HEAD_DIM), F32), pltpu.VMEM((NSA_KV * t, cols), F32),
                        pltpu.VMEM((rows, V7X_LANES), F32), pltpu.VMEM((rows, V7X_LANES), F32),
                        pltpu.VMEM((rows, HEAD_DIM), F32)],
    )
    return pl.pallas_call(
        functools.partial(_nsa_sample_attend_kernel, past=past, n_steps=n_steps, pps=pps),
        grid_spec=grid_spec,
        out_shape=jax.ShapeDtypeStruct((b, t, NSA_Q_W), F32),
        compiler_params=_cparams(2, vmem),
        name="nsa_sample_attend",
    )(pages, *([cache] * pps), proj, sel, ocmp, proj, proj, proj, proj, win_state, proj)


def _nsa_sample(proj, cache, page_ids, kc, vc, win_state, win_index0, past, pps):
    b, t, _ = proj.shape
    ocmp, pslc = _nsa_sample_cmp(proj, kc, vc, past)
    nb = pslc.shape[2]
    qpos = jnp.broadcast_to(past + jnp.arange(t, dtype=jnp.int32), (b * NSA_KV, t)).reshape(1, -1)
    sel_t = _rank_blocks(pslc.reshape(b * NSA_KV * t, nb).T, qpos, TOP_N - 1)
    sel = sel_t.T.reshape(b, NSA_KV * t, nb).astype(BF16)
    return _nsa_sample_attend(cache, page_ids.reshape(-1), proj, sel, ocmp, win_state, win_index0, past, pps)


PROJ_TM = 1024
PROJ_TN = 512
WO_TM = 512
FFN_TM = 512
FFN_TF = 512
FFN_TN = 512
WO_TN = 1024
MEM_TQ = 512
NSA_TQ = 256
NSA_TK = 1024
CMP_PAGES = 16
SAMPLE_PAGES = 8


def kernel(x_prompt, x_sample, cache_nsa_kv, state_nsa_win, state_ret, state_ffn_conv, cache_mem_kv, page_table,
           mem_prompt, norm1_g, nsa_w_in, nsa_cmp_pe, nsa_cmp_w1, nsa_cmp_w2, ret_w_in, ret_gn_g, mem_norm_g,
           w_mem_kv, w_o, norm2_g, ffn_w_in, ffn_conv_w, ffn_conv_b, ffn_w_out, final_norm_g):
    bp, s_len, d = x_prompt.shape
    db, t_len, _ = x_sample.shape
    depth = norm1_g.shape[0]
    n_phys = cache_nsa_kv.shape[1]
    n_mem = mem_prompt.shape[1]
    fdim = ffn_w_out.shape[1]
    ret_heads = state_ret.shape[2]
    tok_w = ret_heads * RET_HEAD_DIM
    past = page_table.shape[1] * PAGE
    wb = state_nsa_win.shape[2]
    keep_p = min(WINDOW, s_len)
    assert cache_nsa_kv.shape[2] == PAGE and tok_w == NSA_Q_W and wb == WINDOW
    assert s_len % RET_CHUNK == 0 and t_len <= RET_CHUNK and t_len % 8 == 0

    xp = x_prompt.reshape(bp * s_len, d)
    xs = x_sample.reshape(db * t_len, d)
    cache = cache_nsa_kv.reshape(-1, KV_PLANES, HEAD_DIM)
    win_state = state_nsa_win.reshape(-1, wb, 2 * NSA_KVW)
    mem2 = mem_prompt.reshape(bp * n_mem, d)
    prompt_pages = jnp.arange(bp * s_len // PAGE, dtype=jnp.int32)
    pos_p = jnp.arange(s_len, dtype=jnp.int32)
    pos_s = past + jnp.minimum(jnp.arange(RET_CHUNK, dtype=jnp.int32), t_len - 1)
    ret_tab_p = _retention_tables(ret_heads, RET_CHUNK, RET_CHUNK, pos_p)
    ret_tab_s = _retention_tables(ret_heads, RET_CHUNK, t_len, pos_s)
    zero_state = jnp.zeros((bp, ret_heads, RET_HEAD_DIM, RET_HEAD_DIM), F32)
    zero_conv = jnp.zeros((bp, CONV_W - 1, fdim), F32)

    kv_p, kv_s, win_p, win_s, ret_p, ret_s, conv_p, conv_s, mem_p = ([] for _ in range(9))
    for i in range(depth):
        mem_kv_p = _norm_matmul(mem2, mem_norm_g[i], w_mem_kv[i].astype(BF16), n_mem, PROJ_TN)
        mem_kv_p = mem_kv_p.reshape(bp, n_mem, 2 * MEM_W)
        mem_p.append(mem_kv_p.reshape(bp, n_mem, 2, MEM_HEADS, HEAD_DIM))
        mem_kv_s = cache_mem_kv[i].reshape(db, n_mem, 2 * MEM_W)
        if i % 2 == 0:
            a = i // 2
            w_tail = _prep_nsa_w(nsa_w_in[a])
            pe, w1, w2 = nsa_cmp_pe[a], nsa_cmp_w1[a], nsa_cmp_w2[a]
            n_main = NSA_MAIN_W // PROJ_TN
            proj_p = _norm_matmul(xp, norm1_g[i], nsa_w_in, PROJ_TM, PROJ_TN, a, n_main, w_tail)
            proj_s = _norm_matmul(xs, norm1_g[i], nsa_w_in, db * t_len, PROJ_TN, a, n_main, w_tail)
            p3 = proj_p.reshape(bp, s_len, NSA_PROJ_W)
            s3 = proj_s.reshape(db, t_len, NSA_PROJ_W)
            rows_lo, rows_hi = NSA_Q_W, NSA_Q_W + NSA_ROWS_W
            kv_p.append(p3[:, :, rows_lo:rows_hi].reshape(bp, s_len, 4, NSA_KV, HEAD_DIM))
            kv_s.append(s3[:, :, rows_lo:rows_hi].reshape(db, t_len, 4, NSA_KV, HEAD_DIM))
            cmp = _even_odd(_compress(proj_p, prompt_pages, pe, w1, w2, CMP_PAGES, col0=NSA_Q_W), bp,
                            planes_first=True)
            tok_p = _nsa_prompt(p3, cmp[0], cmp[1], NSA_COLS, NSA_TQ, NSA_TK)
            page_ids = page_table + a * n_phys
            cmp = _even_odd(_compress(cache, page_ids.reshape(-1), pe, w1, w2, CMP_PAGES), db)
            tok_s = _nsa_sample(s3, cache, page_ids, cmp[0], cmp[1], win_state, a * db, past, SAMPLE_PAGES)
            qm_block = NSA_MAIN_W // MEM_W
            win_p.append(p3[:, s_len - keep_p:, rows_hi:NSA_MAIN_W].reshape(bp, keep_p, 2, NSA_KV, HEAD_DIM))
            new_win = s3[:, :, rows_hi:NSA_MAIN_W].reshape(db, t_len, 2, NSA_KV, HEAD_DIM)
            win_s.append(jnp.concatenate([state_nsa_win[a], new_win], axis=1)[:, -wb:])
        else:
            bl = i // 2
            w_in = ret_w_in[bl].astype(BF16)
            proj_p = _norm_matmul(xp, norm1_g[i], w_in, PROJ_TM, PROJ_TN)
            proj_s = _norm_matmul(xs, norm1_g[i], w_in, db * t_len, PROJ_TN)
            p3 = proj_p.reshape(bp, s_len, -1)
            s3 = proj_s.reshape(db, t_len, -1)
            tok_p, sp = _retention(p3, ret_gn_g[bl], zero_state, *ret_tab_p)
            tok_s, ss = _retention(s3, ret_gn_g[bl], state_ret[bl], *ret_tab_s)
            ret_p.append(sp)
            ret_s.append(ss)
            qm_block = 4 * tok_w // MEM_W
        mem_out_p = _mem_attend(p3, qm_block, mem_kv_p, MEM_TQ)
        mem_out_s = _mem_attend(s3, qm_block, mem_kv_s, t_len)
        wo = w_o[i].astype(BF16)
        xp = _wo(xp, tok_p.reshape(bp * s_len, tok_w), mem_out_p.reshape(bp * s_len, MEM_W), wo[:tok_w], wo[tok_w:],
                 WO_TM, WO_TN)
        xs = _wo(xs, tok_s.reshape(db * t_len, tok_w), mem_out_s.reshape(db * t_len, MEM_W), wo[:tok_w], wo[tok_w:],
                 db * t_len, WO_TN)
        last = i == depth - 1
        ffn_w = (norm2_g[i], ffn_w_in[i].astype(BF16), ffn_conv_w[i], ffn_conv_b[i], ffn_w_out[i].astype(BF16))
        xp, tails = _ffn(xp, *ffn_w, zero_conv, final_norm_g, seq_len=s_len, tm=FFN_TM, tf=FFN_TF, tn=FFN_TN,
                         final_norm=last)
        conv_p.append(tails[s_len // FFN_TM - 1::s_len // FFN_TM])
        xs, up = _ffn(xs, *ffn_w, state_ffn_conv[i], final_norm_g, seq_len=t_len, tm=db * t_len, tf=FFN_TF,
                      tn=FFN_TN, final_norm=last)
        conv_s.append(up.reshape(db, t_len, fdim)[:, t_len - (CONV_W - 1):])
    return (xp.reshape(bp, s_len, d), xs.reshape(db, t_len, d), jnp.stack(kv_p), jnp.stack(kv_s), jnp.stack(win_p),
            jnp.stack(win_s), jnp.stack(ret_p), jnp.stack(ret_s), jnp.stack(conv_p), jnp.stack(conv_s),
            jnp.stack(mem_p))
```

```python
import functools

import jax
import jax.numpy as jnp
from jax import lax
from jax.experimental import pallas as pl
from jax.experimental.pallas import tpu as pltpu

F32 = jnp.float32
BF16 = jnp.bfloat16

HEAD_DIM = 128
MEM_HEADS = 4
NSA_KV = 4
CMP_BLOCK = 32
SLC_BLOCK = 64
TOP_N = 16
WINDOW = 512
FORCE_BONUS = 1.0e4
RET_HEAD_DIM = 256
RET_CHUNK = 128
ROPE_BASE = 10000.0
CONV_W = 3
NORM_EPS = 1e-6
PAGE = 128

V7X_LANES = 128
V7X_VMEM_BYTES = 64 * 1024 * 1024
NEG_BIG = -1e30


def _cparams(n_axes, vmem_bytes):
    limit = min(int(vmem_bytes * 1.25) + (8 << 20), V7X_VMEM_BYTES - (4 << 20))
    return pltpu.CompilerParams(dimension_semantics=("arbitrary",) * n_axes, vmem_limit_bytes=limit)


def _dot(a, b):
    return jnp.dot(a, b, preferred_element_type=F32)


def _dot_nt(a, b):
    return lax.dot_general(a, b, (((1,), (1,)), ((), ())), preferred_element_type=F32)


def _rms(x, g):
    ms = jnp.mean(x * x, axis=-1, keepdims=True)
    return x * lax.rsqrt(ms + NORM_EPS) * g


def _norm_matmul_kernel(x_ref, g_ref, *refs, n_main):
    w_refs, (o_ref, hn_ref) = refs[:-2], refs[-2:]
    j = pl.program_id(1)

    @pl.when(j == 0)
    def _():
        hn_ref[...] = _rms(x_ref[...], g_ref[...]).astype(BF16)

    if len(w_refs) == 1:
        o_ref[...] = _dot(hn_ref[...], w_refs[0][...])
    else:
        @pl.when(j < n_main)
        def _():
            o_ref[...] = _dot(hn_ref[...], w_refs[0][...].astype(BF16))

        @pl.when(j >= n_main)
        def _():
            o_ref[...] = _dot(hn_ref[...], w_refs[1][...])


def _norm_matmul(x, g, w, tm, tn, layer=None, n_main=None, w_tail=None):
    m, k = x.shape
    if w_tail is None:
        n_main, n_tail, weights = w.shape[1] // tn, 0, [w]
        w_specs = [pl.BlockSpec((k, tn), lambda i, j: (0, j))]
    else:
        n_tail, weights = w_tail.shape[1] // tn, [w, w_tail]
        w_specs = [pl.BlockSpec((None, k, tn), lambda i, j: (layer, 0, jnp.minimum(j, n_main - 1))),
                   pl.BlockSpec((k, tn), lambda i, j: (0, jnp.maximum(j - n_main, 0)))]
    vmem = 2 * tm * k * 4 + tm * k * 2 + 2 * (len(weights) + 1) * k * tn * 2 + 2 * tm * tn * 4
    return pl.pallas_call(
        functools.partial(_norm_matmul_kernel, n_main=n_main),
        grid=(m // tm, n_main + n_tail),
        in_specs=[pl.BlockSpec((tm, k), lambda i, j: (i, 0)), pl.BlockSpec((1, k), lambda i, j: (0, 0))] + w_specs,
        out_specs=pl.BlockSpec((tm, tn), lambda i, j: (i, j)),
        out_shape=jax.ShapeDtypeStruct((m, (n_main + n_tail) * tn), F32),
        scratch_shapes=[pltpu.VMEM((tm, k), BF16)],
        compiler_params=_cparams(2, vmem),
        name="norm_matmul",
    )(x, g.reshape(1, k), *weights)


def _wo_kernel(x_ref, tok_ref, mem_ref, wt_ref, wm_ref, o_ref):
    o_ref[...] = (x_ref[...] + _dot(tok_ref[...].astype(BF16), wt_ref[...])
                  + _dot(mem_ref[...].astype(BF16), wm_ref[...]))


def _wo(x, tok, mem, w_tok, w_mem, tm, tn):
    m, d = x.shape
    kt, km = tok.shape[1], mem.shape[1]
    vmem = 2 * (tm * tn * 8 + tm * (kt + km) * 4 + (kt + km) * tn * 2)
    return pl.pallas_call(
        _wo_kernel,
        grid=(m // tm, d // tn),
        in_specs=[
            pl.BlockSpec((tm, tn), lambda i, j: (i, j)),
            pl.BlockSpec((tm, kt), lambda i, j: (i, 0)),
            pl.BlockSpec((tm, km), lambda i, j: (i, 0)),
            pl.BlockSpec((kt, tn), lambda i, j: (0, j)),
            pl.BlockSpec((km, tn), lambda i, j: (0, j)),
        ],
        out_specs=pl.BlockSpec((tm, tn), lambda i, j: (i, j)),
        out_shape=jax.ShapeDtypeStruct((m, d), F32),
        compiler_params=_cparams(2, vmem),
        name="wo_residual",
    )(x, tok, mem, w_tok, w_mem)


def _ffn_kernel(x_ref, g_ref, wa_ref, wg_ref, cw_ref, cb_ref, wo_ref, fg_ref, *refs,
                tiles_per_seq, seq_len, n_f, tn, final_norm):
    multi_seq = tiles_per_seq == 0
    prev_refs, (y_ref, tail_ref, hn_ref, act_ref, carry_ref) = refs[:-5], refs[-5:]
    i = pl.program_id(0)
    s = pl.program_id(1)
    tm, d = x_ref.shape

    @pl.when(s == 0)
    def _():
        hn_ref[...] = _rms(x_ref[...], g_ref[...]).astype(BF16)

    @pl.when(s < n_f)
    def _():
        hn = hn_ref[...]
        a = _dot(hn, wa_ref[...])
        gv = _dot(hn, wg_ref[...])
        row = lax.broadcasted_iota(jnp.int32, a.shape, 0)
        if multi_seq:
            t = row % seq_len
            a1 = jnp.where(t == 0, prev_refs[0][...], pltpu.roll(a, 1, axis=0))
            a2 = jnp.where(t < 2, prev_refs[1][...], pltpu.roll(a, 2, axis=0))
            tail_ref[...] = a
        else:
            prev = jnp.where((i % tiles_per_seq) == 0, prev_refs[0][0], carry_ref[s])
            a1 = jnp.where(row == 0, prev[1:2], pltpu.roll(a, 1, axis=0))
            a2 = jnp.where(row == 0, prev[0:1], jnp.where(row == 1, prev[1:2], pltpu.roll(a, 2, axis=0)))
            carry_ref[s] = a[tm - 2:tm]
            tail_ref[0] = a[tm - 2:tm]
        cw = cw_ref[...]
        ac = cb_ref[...] + a2 * cw[0:1] + a1 * cw[1:2] + a * cw[2:3]
        act_ref[s] = (ac * jax.nn.sigmoid(ac) * gv).astype(BF16)

    for n in range(d // tn):
        @pl.when(s == n_f + n)
        def _(n=n):
            act = jnp.concatenate([act_ref[k] for k in range(n_f)], axis=1)
            cols = slice(n * tn, (n + 1) * tn)
            y_ref[:, cols] = x_ref[:, cols] + _dot(act, wo_ref[...])

    if final_norm:
        @pl.when(s == n_f + d // tn - 1)
        def _():
            y_ref[...] = _rms(y_ref[...], fg_ref[...])


def _ffn(x, norm_g, w_in, conv_w, conv_b, w_out, buf, final_g, *, seq_len, tm, tf, tn, final_norm):
    m, d = x.shape
    fdim = w_out.shape[0]
    nf, nn = fdim // tf, d // tn
    multi_seq = tm > seq_len
    tiles_per_seq = 0 if multi_seq else seq_len // tm
    fa = lambda s: jnp.minimum(s, nf - 1)
    fb = lambda s: jnp.maximum(s - nf, 0)
    if multi_seq:
        n_seq = m // seq_len
        first = jnp.zeros((n_seq, seq_len, fdim), F32).at[:, 0].set(buf[:, 1])
        both = first.at[:, 0].set(buf[:, 0]).at[:, 1].set(buf[:, 1])
        prev = [first.reshape(m, fdim), both.reshape(m, fdim)]
        prev_specs = [pl.BlockSpec((tm, tf), lambda i, s: (i, fa(s)))] * 2
        tail_spec = pl.BlockSpec((tm, tf), lambda i, s: (i, fa(s)))
        tail_shape = jax.ShapeDtypeStruct((m, fdim), F32)
    else:
        prev = [buf]
        prev_specs = [pl.BlockSpec((1, CONV_W - 1, tf), lambda i, s: (i // tiles_per_seq, 0, fa(s)))]
        tail_spec = pl.BlockSpec((1, CONV_W - 1, tf), lambda i, s: (i, 0, fa(s)))
        tail_shape = jax.ShapeDtypeStruct((m // tm, CONV_W - 1, fdim), F32)
    vmem = (3 * tm * d * 4 + tm * d * 2 + 2 * tm * fdim * 2 + 2 * (2 * d * tf * 2 + fdim * tn * 2)
            + 10 * tm * tf * 4 + nf * 8 * tf * 4)
    kern = functools.partial(_ffn_kernel, tiles_per_seq=tiles_per_seq, seq_len=seq_len, n_f=nf, tn=tn,
                             final_norm=final_norm)
    return pl.pallas_call(
        kern,
        grid=(m // tm, nf + nn),
        in_specs=[
            pl.BlockSpec((tm, d), lambda i, s: (i, 0)),
            pl.BlockSpec((1, d), lambda i, s: (0, 0)),
            pl.BlockSpec((d, tf), lambda i, s: (0, fa(s))),
            pl.BlockSpec((d, tf), lambda i, s: (0, fa(s) + nf)),
            pl.BlockSpec((CONV_W, tf), lambda i, s: (0, fa(s))),
            pl.BlockSpec((1, tf), lambda i, s: (0, fa(s))),
            pl.BlockSpec((fdim, tn), lambda i, s: (0, fb(s))),
            pl.BlockSpec((1, d), lambda i, s: (0, 0)),
        ] + prev_specs,
        out_specs=[pl.BlockSpec((tm, d), lambda i, s: (i, 0), pipeline_mode=pl.Buffered(1)), tail_spec],
        out_shape=[jax.ShapeDtypeStruct((m, d), F32), tail_shape],
        scratch_shapes=[pltpu.VMEM((tm, d), BF16), pltpu.VMEM((nf, tm, tf), BF16),
                        pltpu.VMEM((nf, CONV_W - 1, tf), F32)],
        compiler_params=_cparams(2, vmem),
        name="conv_ffn",
    )(x, norm_g.reshape(1, d), w_in, w_in, conv_w, conv_b.reshape(1, fdim), w_out, final_g.reshape(1, d), *prev)


def _mem_kernel(q_ref, kv_ref, o_ref):
    q = q_ref[0] * (HEAD_DIM ** -0.5)
    kv = kv_ref[0]
    width = MEM_HEADS * HEAD_DIM
    outs = []
    for h in range(MEM_HEADS):
        lo = h * HEAD_DIM
        qh = q[:, lo:lo + HEAD_DIM].astype(BF16)
        kh = kv[:, lo:lo + HEAD_DIM].astype(BF16)
        vh = kv[:, width + lo:width + lo + HEAD_DIM].astype(BF16)
        s = _dot_nt(qh, kh)
        e = jnp.exp(s - jnp.max(s, axis=-1, keepdims=True))
        p = e / jnp.sum(e, axis=-1, keepdims=True)
        outs.append(_dot(p.astype(BF16), vh))
    o_ref[0] = jnp.concatenate(outs, axis=-1)


def _mem_attend(proj, qm_col_block, mem_kv, tq):
    b, t, _ = proj.shape
    width = MEM_HEADS * HEAD_DIM
    n_mem = mem_kv.shape[1]
    vmem = 2 * (2 * tq * width * 4 + n_mem * 2 * width * 4) + 8 * tq * n_mem * 4
    return pl.pallas_call(
        _mem_kernel,
        grid=(b, t // tq),
        in_specs=[
            pl.BlockSpec((1, tq, width), lambda bi, i: (bi, i, qm_col_block)),
            pl.BlockSpec((1, n_mem, 2 * width), lambda bi, i: (bi, 0, 0)),
        ],
        out_specs=pl.BlockSpec((1, tq, width), lambda bi, i: (bi, i, 0)),
        out_shape=jax.ShapeDtypeStruct((b, t, width), F32),
        compiler_params=_cparams(2, vmem),
        name="mem_attend",
    )(proj, mem_kv)


def _ret_kernel(q_ref, k_ref, v_ref, gate_ref, cos_ref, sin_ref, dmat_ref, qd_ref, kd_ref, cd_ref,
                gn_ref, s0_ref, tok_ref, s_out_ref, s_ref, *, n_chunks):
    c = pl.program_id(1)
    n_heads, dk = s_ref.shape[0], s_ref.shape[1]

    @pl.when(c == 0)
    def _():
        s_ref[...] = s0_ref[0]

    half = dk // 2
    cos = cos_ref[...]
    sin = sin_ref[...]
    rows = q_ref.shape[1]
    c_len = cos.shape[0]

    def load(ref, h):
        x = ref[0, :, h * dk:(h + 1) * dk]
        return x if rows == c_len else jnp.concatenate([x, jnp.zeros((c_len - rows, dk), F32)], axis=0)

    def rot(x):
        x1, x2 = x[:, :half], x[:, half:]
        return jnp.concatenate([x1 * cos - x2 * sin, x1 * sin + x2 * cos], axis=-1)

    for h in range(n_heads):
        q = rot(load(q_ref, h))
        k = rot(load(k_ref, h)) * (dk ** -0.5)
        v = load(v_ref, h).astype(BF16)
        qb = q.astype(BF16)
        s_old = s_ref[h]
        inner = _dot_nt(qb, k.astype(BF16)) * dmat_ref[h]
        o = _dot(inner.astype(BF16), v) + _dot(qb, s_old.astype(BF16)) * qd_ref[h]
        kd = (k * kd_ref[h]).astype(BF16)
        s_ref[h] = s_old * cd_ref[h] + _dot(kd.T, v)

        mu = jnp.mean(o, axis=-1, keepdims=True)
        dev = o - mu
        var = jnp.mean(dev * dev, axis=-1, keepdims=True)
        y = dev * lax.rsqrt(var + NORM_EPS) * gn_ref[h]
        gate = gate_ref[0, :, h * dk:(h + 1) * dk]
        tok_ref[0, :, h * dk:(h + 1) * dk] = gate * jax.nn.sigmoid(gate) * y[:rows]

    @pl.when(c == n_chunks - 1)
    def _():
        s_out_ref[0] = s_ref[...]


def _retention(proj, gn_g, s0, cos, sin, dmat, q_decay, k_decay, c_decay):
    b, t, _ = proj.shape
    h = s0.shape[1]
    dk = RET_HEAD_DIM
    c = dmat.shape[1]
    rows = min(t, c)
    n = t // rows
    vmem = 2 * (5 * rows * h * dk * 4 + h * c * c * 4 + 2 * h * c * 128 * 4 + 2 * h * dk * dk * 4) + 3 * h * dk * dk * 4
    blk = lambda group: pl.BlockSpec((1, rows, h * dk), lambda bi, ci: (bi, ci, group))
    whole = lambda shape: pl.BlockSpec(shape, lambda bi, ci: (0,) * len(shape))
    return pl.pallas_call(
        functools.partial(_ret_kernel, n_chunks=n),
        grid=(b, n),
        in_specs=[
            blk(0), blk(1), blk(2), blk(3),
            pl.BlockSpec((c, dk // 2), lambda bi, ci: (ci, 0)),
            pl.BlockSpec((c, dk // 2), lambda bi, ci: (ci, 0)),
            whole((h, c, c)), whole((h, c, 1)), whole((h, c, 1)), whole((h, 1, dk)), whole((h, 1, dk)),
            pl.BlockSpec((1, h, dk, dk), lambda bi, ci: (bi, 0, 0, 0)),
        ],
        out_specs=[
            pl.BlockSpec((1, rows, h * dk), lambda bi, ci: (bi, ci, 0)),
            pl.BlockSpec((1, h, dk, dk), lambda bi, ci: (bi, 0, 0, 0)),
        ],
        out_shape=[jax.ShapeDtypeStruct((b, t, h * dk), F32),
                   jax.ShapeDtypeStruct((b, h, dk, dk), F32)],
        scratch_shapes=[pltpu.VMEM((h, dk, dk), F32)],
        compiler_params=_cparams(2, vmem),
        name="retention",
    )(proj, proj, proj, proj, cos, sin, dmat, q_decay, k_decay, c_decay, gn_g.reshape(h, 1, dk), s0)


def _retention_tables(n_heads, c_pad, c_real, pos):
    log_g = jnp.log1p(-jnp.exp2(-5.0 - jnp.arange(n_heads, dtype=F32)))
    i = jnp.arange(c_pad, dtype=F32)
    live = i < c_real
    diff = i[:, None] - i[None, :]
    dmat = jnp.where((diff >= 0) & live[:, None] & live[None, :],
                     jnp.exp(jnp.maximum(diff, 0.0)[None] * log_g[:, None, None]), 0.0)
    q_decay = jnp.exp((i + 1.0)[None, :] * log_g[:, None])[..., None]
    k_decay = jnp.where(live[None, :], jnp.exp((c_real - 1.0 - i)[None, :] * log_g[:, None]), 0.0)[..., None]
    c_decay = jnp.broadcast_to(jnp.exp(c_real * log_g)[:, None, None], (n_heads, 1, RET_HEAD_DIM))
    half = RET_HEAD_DIM // 2
    inv = ROPE_BASE ** (-jnp.arange(half, dtype=F32) / half)
    ang = pos.astype(F32)[:, None] * inv[None, :]
    return jnp.cos(ang), jnp.sin(ang), dmat, q_decay, k_decay, c_decay


KV_PLANES = 4 * NSA_KV
CMP_PLANES = 2 * NSA_KV


def _compress_kernel(plist_ref, src_ref, pe_ref, w1_ref, w2_ref, o_ref, buf_ref, lhs_ref, sem_ref, *,
                     pages, n_steps, col0):
    s = pl.program_id(0)
    slot = s % 2
    bpp = PAGE // CMP_BLOCK
    nblk = pages * bpp
    rows = nblk * CMP_PLANES

    def copies(step, to_slot):
        cps = []
        for p in range(pages):
            tok0 = plist_ref[step * pages + p] * PAGE
            for n in range(bpp):
                toks = pl.ds(tok0 + n * CMP_BLOCK, CMP_BLOCK)
                if col0 is None:
                    src = src_ref.at[toks, pl.ds(0, CMP_PLANES), :]
                else:
                    src = src_ref.at[toks, pl.ds(col0, CMP_PLANES * HEAD_DIM)]
                cps.append(pltpu.make_async_copy(src, buf_ref.at[to_slot, :, p * bpp + n], sem_ref.at[to_slot]))
        return cps

    @pl.when(s == 0)
    def _():
        for cp in copies(0, 0):
            cp.start()

    @pl.when(s + 1 < n_steps)
    def _():
        for cp in copies(s + 1, 1 - slot):
            cp.start()

    for cp in copies(s, slot):
        cp.wait()

    row = lax.broadcasted_iota(jnp.int32, (rows, HEAD_DIM), 0)
    if col0 is None:
        for c in range(CMP_BLOCK):
            x = buf_ref[slot, c] + pe_ref[c]
            lhs_ref[:, c * HEAD_DIM:(c + 1) * HEAD_DIM] = x.reshape(rows, HEAD_DIM).astype(BF16)
        is_key = (row % CMP_PLANES) < NSA_KV
    else:
        for c in range(CMP_BLOCK):
            x = buf_ref[slot, c]
            for j in range(CMP_PLANES):
                xj = x[:, j * HEAD_DIM:(j + 1) * HEAD_DIM] + pe_ref[c, j:j + 1, :]
                lhs_ref[j * nblk:(j + 1) * nblk, c * HEAD_DIM:(c + 1) * HEAD_DIM] = xj.astype(BF16)
        is_key = row < NSA_KV * nblk
    h = _dot(lhs_ref[...], w1_ref[...])
    h = jax.nn.gelu(jnp.where(is_key, h[:, :HEAD_DIM], h[:, HEAD_DIM:]))
    out = _dot(h.astype(BF16), w2_ref[...])
    o_ref[...] = jnp.where(is_key, out[:, :HEAD_DIM], out[:, HEAD_DIM:]).reshape(o_ref.shape)


def _compress(src, plist, pe, w1, w2, pages, col0=None):
    n_pages = plist.shape[0]
    nblk = pages * (PAGE // CMP_BLOCK)
    n_total = n_pages * (PAGE // CMP_BLOCK)
    rows = nblk * CMP_PLANES
    k_dim = CMP_BLOCK * HEAD_DIM
    pe8 = jnp.repeat(pe.transpose(1, 0, 2), NSA_KV, axis=1)
    w1b = jnp.concatenate([w1[0], w1[1]], axis=1).astype(BF16)
    w2b = jnp.concatenate([w2[0], w2[1]], axis=1).astype(BF16)
    if col0 is None:
        buf_shape = (2, CMP_BLOCK, nblk, CMP_PLANES, HEAD_DIM)
        out_spec = pl.BlockSpec((nblk, CMP_PLANES, HEAD_DIM), lambda s, pr: (s, 0, 0))
        out_shape = (n_total, CMP_PLANES, HEAD_DIM)
    else:
        buf_shape = (2, CMP_BLOCK, nblk, CMP_PLANES * HEAD_DIM)
        out_spec = pl.BlockSpec((CMP_PLANES, nblk, HEAD_DIM), lambda s, pr: (0, s, 0))
        out_shape = (CMP_PLANES, n_total, HEAD_DIM)
    vmem = 2 * CMP_BLOCK * rows * HEAD_DIM * 4 + rows * k_dim * 2 + 2 * k_dim * 2 * HEAD_DIM * 2 + 8 * rows * HEAD_DIM * 4
    grid_spec = pltpu.PrefetchScalarGridSpec(
        num_scalar_prefetch=1,
        grid=(n_pages // pages,),
        in_specs=[
            pl.BlockSpec(memory_space=pl.ANY),
            pl.BlockSpec((CMP_BLOCK, CMP_PLANES, HEAD_DIM), lambda s, pr: (0, 0, 0)),
            pl.BlockSpec((k_dim, 2 * HEAD_DIM), lambda s, pr: (0, 0)),
            pl.BlockSpec((HEAD_DIM, 2 * HEAD_DIM), lambda s, pr: (0, 0)),
        ],
        out_specs=out_spec,
        scratch_shapes=[pltpu.VMEM(buf_shape, F32), pltpu.VMEM((rows, k_dim), BF16), pltpu.SemaphoreType.DMA((2,))],
    )
    return pl.pallas_call(
        functools.partial(_compress_kernel, pages=pages, n_steps=n_pages // pages, col0=col0),
        grid_spec=grid_spec,
        out_shape=jax.ShapeDtypeStruct(out_shape, F32),
        compiler_params=_cparams(1, vmem),
        name="nsa_compress",
    )(plist, src, pe8, w1b, w2b)


def _even_odd(c, n_seq, planes_first=False):
    d = c.shape[-1]
    nc = c.shape[1 if planes_first else 0] // n_seq
    if planes_first:
        c = c.reshape(2, NSA_KV, n_seq, nc // 2, 2, d).transpose(0, 1, 2, 4, 3, 5)
    else:
        c = c.reshape(n_seq, nc // 2, 2, 2, NSA_KV, d).transpose(3, 4, 0, 2, 1, 5)
    return c.reshape(2, NSA_KV, n_seq, nc, d).astype(BF16)


def _cmp_attend(qb, kc, vc, qpos_col):
    nc = kc.shape[0]
    lane = lax.broadcasted_iota(jnp.int32, (1, nc), 1)
    blk = jnp.where(lane < nc // 2, 2 * lane, 2 * (lane - nc // 2) + 1)
    vis = (blk * CMP_BLOCK + (CMP_BLOCK - 1)) <= qpos_col
    s = jnp.where(vis, _dot_nt(qb, kc), -jnp.inf)
    m = jnp.max(s, axis=-1, keepdims=True)
    m = jnp.where(m > -jnp.inf, m, 0.0)
    e = jnp.where(vis, jnp.exp(s - m), 0.0)
    p = e / jnp.maximum(jnp.sum(e, axis=-1, keepdims=True), 1e-30)
    return _dot(p.astype(BF16), vc), p


def _select_blocks(p_t, qpos_row, n_sel, m_hi, score_ref):
    nb, nq = p_t.shape
    blk = lax.broadcasted_iota(jnp.int32, (nb, nq), 0)
    valid = blk * SLC_BLOCK <= qpos_row
    cur = qpos_row // SLC_BLOCK
    forced = (blk == 0) | (blk == cur) | (blk == cur - 1)
    score = jnp.where(valid, p_t + jnp.where(forced, FORCE_BONUS, 0.0), -jnp.inf)
    score_ref[...] = score
    rows = min(64, nb)
    outs = []
    for r0 in range(0, nb, rows):
        sc = score[r0:r0 + rows]
        bk = blk[r0:r0 + rows]

        def body(mi, rank, sc=sc, bk=bk):
            other = score_ref[pl.ds(mi, 1), :]
            ge = jnp.where(other >= sc, 1.0, 0.0)
            gt = jnp.where(other > sc, 1.0, 0.0)
            return rank + jnp.where(bk > mi, ge, gt)

        rank = lax.fori_loop(0, m_hi, body, jnp.zeros_like(sc))
        outs.append(jnp.where((rank < n_sel) & valid[r0:r0 + rows], 1.0, 0.0))
    return jnp.concatenate(outs, axis=0) if len(outs) > 1 else outs[0]


def _lane_tile(x, n_lanes):
    reps = n_lanes // x.shape[1]
    return x if reps == 1 else jnp.concatenate([x] * reps, axis=1)


def _softmax_step(s, vis, v, m_ref, l_ref, acc_ref, r0, value_shift=0, bias=None):
    rows, n_keys = s.shape
    sl = pl.ds(r0, rows)
    s = jnp.where(vis, s, NEG_BIG) if bias is None else s + bias
    m_old = m_ref[sl, :]
    m_new = jnp.maximum(m_old, jnp.max(s, axis=-1, keepdims=True))
    alpha = jnp.exp(m_old - m_new)
    p = jnp.exp(s - _lane_tile(m_new, n_keys))
    l_ref[sl, :] = alpha * l_ref[sl, :] + jnp.sum(p, axis=-1, keepdims=True)
    pv = pltpu.roll(p, value_shift, axis=1) if value_shift else p
    acc_ref[sl, :] = alpha * acc_ref[sl, :] + _dot(pv.astype(BF16), v)
    m_ref[sl, :] = m_new


def _softmax_finish(m_ref, l_ref, acc_ref):
    seen = m_ref[...] > 0.5 * NEG_BIG
    return jnp.where(seen, acc_ref[...] / jnp.maximum(l_ref[...], 1e-30), 0.0)


def _softmax_once(s, vis, v):
    groups = s.shape[1] // V7X_LANES
    s = jnp.where(vis, s, NEG_BIG)
    part = lambda x, c: x[:, c * V7X_LANES:(c + 1) * V7X_LANES]
    m = functools.reduce(jnp.maximum, [part(s, c) for c in range(groups)])
    m = jnp.max(m, axis=-1, keepdims=True)
    e = jnp.exp(s - m)
    l = functools.reduce(jnp.add, [part(e, c) for c in range(groups)])
    l = jnp.maximum(jnp.sum(l, axis=-1, keepdims=True), 1e-30)
    return jnp.where(m > 0.5 * NEG_BIG, _dot(e.astype(BF16), v) / l, 0.0)


def _block_expander(n_blocks, key):
    blk = lax.broadcasted_iota(jnp.int32, (n_blocks, key.shape[1]), 0)
    return jnp.where(blk == key // SLC_BLOCK, 1.0, 0.0).astype(BF16)


def _nsa_prompt_kernel(q_ref, kc_ref, vc_ref, ks_ref, vs_ref, kw_ref, vw_ref, gate_ref, tok_ref,
                       score_ref, sel_ref, s_ref, m_ref, l_ref, acc_ref, *, tk):
    i = pl.program_id(2)
    tq = q_ref.shape[1]
    r = q_ref.shape[2] // HEAD_DIM
    nb = kc_ref.shape[2] // (SLC_BLOCK // CMP_BLOCK)
    nbp = -(-nb // V7X_LANES) * V7X_LANES
    t0 = i * tq
    q = q_ref[0] * (HEAD_DIM ** -0.5)
    qb = jnp.concatenate([q[:, h * HEAD_DIM:(h + 1) * HEAD_DIM] for h in range(r)], axis=0).astype(BF16)
    tcol = t0 + lax.broadcasted_iota(jnp.int32, (tq, 1), 0)
    trow = t0 + lax.broadcasted_iota(jnp.int32, (1, tq), 1)

    o_cmp, p = _cmp_attend(qb, kc_ref[0, 0], vc_ref[0, 0], jnp.concatenate([tcol] * r, axis=0))
    p_grp = p[0:tq]
    for h in range(1, r):
        p_grp = p_grp + p[h * tq:(h + 1) * tq]
    p_pair = p_grp + pltpu.roll(p_grp, nb, axis=1)
    m_hi = jnp.minimum((t0 + tq - 1) // SLC_BLOCK + 1, nb)
    sel_t = _select_blocks(p_pair.T[:nb], trow, TOP_N, m_hi, score_ref)
    if nbp > nb:
        sel_t = jnp.concatenate([sel_t, jnp.zeros((nbp - nb, tq), F32)], axis=0)
    sel = sel_t.T.astype(BF16)
    bpt = tk // SLC_BLOCK
    for jj in range(sel_ref.shape[0]):
        sel_ref[jj] = sel[:, jj * bpt:(jj + 1) * bpt]
    local_key = lax.broadcasted_iota(jnp.int32, (1, tk), 1)

    n_kt = (t0 + tq + tk - 1) // tk
    lane_groups = tk // V7X_LANES
    m_ref[...] = jnp.full(m_ref.shape, NEG_BIG, F32)

    def score_tile(j, carry):
        k0 = pl.multiple_of(j * tk, tk)
        s = _dot_nt(qb, ks_ref[0, pl.ds(k0, tk), :].astype(BF16))
        picked = _dot(sel_ref[j], _block_expander(bpt, local_key))
        kpos = k0 + lax.broadcasted_iota(jnp.int32, (tq, tk), 1)
        vis = (picked > 0.5) & (kpos <= tcol)
        for h in range(r):
            rows = pl.ds(h * tq, tq)
            sh = jnp.where(vis, s[h * tq:(h + 1) * tq], NEG_BIG)
            s_ref[j, rows, :] = sh
            m = m_ref[rows, :]
            for c in range(lane_groups):
                m = jnp.maximum(m, sh[:, c * V7X_LANES:(c + 1) * V7X_LANES])
            m_ref[rows, :] = m
        return carry

    lax.fori_loop(0, n_kt, score_tile, 0)
    m_row = jnp.max(m_ref[...], axis=-1, keepdims=True)
    m_ref[...] = jnp.broadcast_to(m_row, m_ref.shape)
    l_ref[...] = jnp.zeros(l_ref.shape, F32)
    acc_ref[...] = jnp.zeros(acc_ref.shape, F32)

    def value_tile(j, carry):
        k0 = pl.multiple_of(j * tk, tk)
        p = jnp.exp(s_ref[j] - _lane_tile(m_ref[...], tk))
        l = l_ref[...]
        for c in range(lane_groups):
            l = l + p[:, c * V7X_LANES:(c + 1) * V7X_LANES]
        l_ref[...] = l
        acc_ref[...] += _dot(p.astype(BF16), vs_ref[0, pl.ds(k0, tk), :].astype(BF16))
        return carry

    lax.fori_loop(0, n_kt, value_tile, 0)
    denom = jnp.maximum(jnp.sum(l_ref[...], axis=-1, keepdims=True), 1e-30)
    o_slc = jnp.where(m_row > 0.5 * NEG_BIG, acc_ref[...] / denom, 0.0)

    span = WINDOW + tq
    w0 = pl.multiple_of(jnp.maximum(t0 + tq - span, 0), tq)
    kw = kw_ref[0, pl.ds(w0, span), :].astype(BF16)
    vw = vw_ref[0, pl.ds(w0, span), :].astype(BF16)
    s = _dot_nt(qb, kw)
    kpos = w0 + lax.broadcasted_iota(jnp.int32, (tq, span), 1)
    vis = (kpos <= tcol) & (kpos > tcol - WINDOW)
    o_win = [_softmax_once(s[h * tq:(h + 1) * tq], vis, vw) for h in range(r)]

    gt = jax.nn.sigmoid(gate_ref[0])
    outs = []
    for h in range(r):
        rows = slice(h * tq, (h + 1) * tq)
        outs.append(gt[:, 3 * h:3 * h + 1] * o_cmp[rows] + gt[:, 3 * h + 1:3 * h + 2] * o_slc[rows]
                    + gt[:, 3 * h + 2:3 * h + 3] * o_win[h])
    tok_ref[0] = jnp.concatenate(outs, axis=-1)


def _nsa_prompt(proj, kc, vc, cols, tq, tk):
    b, t, _ = proj.shape
    g = kc.shape[0]
    nc = kc.shape[2]
    r = 3
    nb = nc // (SLC_BLOCK // CMP_BLOCK)
    assert t >= WINDOW + tq and t % tk == 0 and tk % tq == 0 and nb % 64 == 0 and nc <= V7X_LANES
    kv_spec = lambda c0: pl.BlockSpec((1, t, HEAD_DIM), lambda bi, gi, i: (bi, 0, c0 + gi))
    cmp_spec = pl.BlockSpec((1, 1, nc, HEAD_DIM), lambda bi, gi, i: (gi, bi, 0, 0))
    vmem = 2 * 4 * t * HEAD_DIM * 4 + r * tq * t * 4 + 24 * r * tq * max(tk, WINDOW + tq) * 4
    return pl.pallas_call(
        functools.partial(_nsa_prompt_kernel, tk=tk),
        grid=(b, g, t // tq),
        in_specs=[
            pl.BlockSpec((1, tq, r * HEAD_DIM), lambda bi, gi, i: (bi, i, cols["q"] // r + gi)),
            cmp_spec, cmp_spec,
            kv_spec(cols["ks"]), kv_spec(cols["vs"]), kv_spec(cols["kw"]), kv_spec(cols["vw"]),
            pl.BlockSpec((1, tq, HEAD_DIM), lambda bi, gi, i: (bi, i, cols["gates"] + gi)),
        ],
        out_specs=pl.BlockSpec((1, tq, r * HEAD_DIM), lambda bi, gi, i: (bi, i, gi)),
        out_shape=jax.ShapeDtypeStruct((b, t, g * r * HEAD_DIM), F32),
        scratch_shapes=[pltpu.VMEM((nb, tq), F32), pltpu.VMEM((t // tk, tq, tk // SLC_BLOCK), BF16),
                        pltpu.VMEM((t // tk, r * tq, tk), F32),
                        pltpu.VMEM((r * tq, V7X_LANES), F32), pltpu.VMEM((r * tq, V7X_LANES), F32),
                        pltpu.VMEM((r * tq, HEAD_DIM), F32)],
        compiler_params=_cparams(3, vmem),
        name="nsa_prompt",
    )(proj, kc, vc, proj, proj, proj, proj, proj)


NSA_HEADS = 12
NSA_KVW = NSA_KV * HEAD_DIM
NSA_Q_W = NSA_HEADS * HEAD_DIM
NSA_ROWS_W = 4 * NSA_KVW
NSA_WIN_W = 2 * NSA_KVW
MEM_W = MEM_HEADS * HEAD_DIM
NSA_GATES = 3 * NSA_HEADS
NSA_MAIN_W = NSA_Q_W + NSA_ROWS_W + NSA_WIN_W
NSA_PROJ_W = NSA_MAIN_W + MEM_W + NSA_KV * V7X_LANES
NSA_COLS = {
    "q": 0,
    "ks": (NSA_Q_W + 2 * NSA_KVW) // V7X_LANES,
    "vs": (NSA_Q_W + 3 * NSA_KVW) // V7X_LANES,
    "kw": (NSA_Q_W + NSA_ROWS_W) // V7X_LANES,
    "vw": (NSA_Q_W + NSA_ROWS_W + NSA_KVW) // V7X_LANES,
    "gates": (NSA_MAIN_W + MEM_W) // V7X_LANES,
}


def _prep_nsa_w(w):
    d = w.shape[0]
    gates = w[:, NSA_MAIN_W:NSA_MAIN_W + NSA_GATES].reshape(d, NSA_KV, NSA_GATES // NSA_KV)
    gates = jnp.pad(gates, ((0, 0), (0, 0), (0, V7X_LANES - NSA_GATES // NSA_KV))).reshape(d, NSA_KV * V7X_LANES)
    return jnp.concatenate([w[:, NSA_MAIN_W + NSA_GATES:], gates], axis=1).astype(BF16)


def _stack_heads(q, n_heads):
    return jnp.concatenate([q[:, h * HEAD_DIM:(h + 1) * HEAD_DIM] for h in range(n_heads)], axis=0)


def _nsa_sample_cmp_kernel(q_ref, kc_ref, vc_ref, ocmp_ref, pslc_ref, *, past):
    t = q_ref.shape[1]
    g_n = kc_ref.shape[0]
    r = q_ref.shape[2] // HEAD_DIM // g_n
    nb = kc_ref.shape[2] // (SLC_BLOCK // CMP_BLOCK)
    q = q_ref[0] * (HEAD_DIM ** -0.5)
    qpos = past + lax.broadcasted_iota(jnp.int32, (t, 1), 0)
    qpos_r = jnp.concatenate([qpos] * r, axis=0)
    for g in range(g_n):
        qb = _stack_heads(q[:, g * r * HEAD_DIM:(g + 1) * r * HEAD_DIM], r).astype(BF16)
        o, p = _cmp_attend(qb, kc_ref[g, 0], vc_ref[g, 0], qpos_r)
        ocmp_ref[0, g * r * t:(g + 1) * r * t, :] = o
        p_grp = p[0:t]
        for h in range(1, r):
            p_grp = p_grp + p[h * t:(h + 1) * t]
        p_pair = p_grp + pltpu.roll(p_grp, nb, axis=1)
        pslc_ref[0, g * t:(g + 1) * t, :] = p_pair[:, :nb]


def _nsa_sample_cmp(proj, kc, vc, past):
    b, t, _ = proj.shape
    g, _, nc, _ = kc.shape
    nb = nc // (SLC_BLOCK // CMP_BLOCK)
    cmp_spec = pl.BlockSpec((g, 1, nc, HEAD_DIM), lambda bi: (0, bi, 0, 0))
    vmem = 2 * (t * NSA_Q_W * 4 + 2 * g * nc * HEAD_DIM * 2) + 64 * nc * 4 * 8
    return pl.pallas_call(
        functools.partial(_nsa_sample_cmp_kernel, past=past),
        grid=(b,),
        in_specs=[pl.BlockSpec((1, t, NSA_Q_W), lambda bi: (bi, 0, 0)), cmp_spec, cmp_spec],
        out_specs=[pl.BlockSpec((1, NSA_HEADS * t, HEAD_DIM), lambda bi: (bi, 0, 0)),
                   pl.BlockSpec((1, g * t, nb), lambda bi: (bi, 0, 0))],
        out_shape=[jax.ShapeDtypeStruct((b, NSA_HEADS * t, HEAD_DIM), F32),
                   jax.ShapeDtypeStruct((b, g * t, nb), F32)],
        compiler_params=_cparams(1, vmem),
        name="nsa_sample_cmp",
    )(proj, kc, vc)


def _rank_kernel(p_ref, qpos_ref, sel_ref, score_ref, *, n_sel):
    sel_ref[...] = _select_blocks(p_ref[...], qpos_ref[...], n_sel, p_ref.shape[0], score_ref)


def _rank_blocks(p_t, qpos, n_sel):
    nb, nq = p_t.shape
    tq = V7X_LANES
    return pl.pallas_call(
        functools.partial(_rank_kernel, n_sel=n_sel),
        grid=(nq // tq,),
        in_specs=[pl.BlockSpec((nb, tq), lambda i: (0, i)), pl.BlockSpec((1, tq), lambda i: (0, i))],
        out_specs=pl.BlockSpec((nb, tq), lambda i: (0, i)),
        out_shape=jax.ShapeDtypeStruct((nb, nq), F32),
        scratch_shapes=[pltpu.VMEM((nb, tq), F32)],
        compiler_params=_cparams(1, 8 * nb * tq * 4),
        name="nsa_rank",
    )(p_t, qpos)


def _nsa_sample_attend_kernel(pages_ref, *refs, past, n_steps, pps):
    kv_refs = refs[:pps]
    (q_ref, sel_ref, ocmp_ref, ksn_ref, vsn_ref, kwn_ref, vwn_ref, wst_ref, gate_ref, tok_ref,
     qs_ref, own_ref, m_ref, l_ref, acc_ref) = refs[pps:]
    j = pl.program_id(1)
    t = q_ref.shape[1]
    g_n = NSA_KV
    r = NSA_HEADS // g_n
    rg = r * t
    span = pps * PAGE
    half = KV_PLANES // 2
    cols = span * half

    @pl.when(j == 0)
    def _():
        qs_ref[...] = _stack_heads(q_ref[0] * (HEAD_DIM ** -0.5), NSA_HEADS)
        m_ref[...] = jnp.full(m_ref.shape, NEG_BIG, F32)
        l_ref[...] = jnp.zeros(l_ref.shape, F32)
        acc_ref[...] = jnp.zeros(acc_ref.shape, F32)
        lane = lax.broadcasted_iota(jnp.int32, (g_n * t, cols), 1)
        own = lax.broadcasted_iota(jnp.int32, (g_n * t, cols), 0) // t
        own_ref[...] = jnp.where((lane % half) == own, 1.0, 0.0)

    kvb = jnp.concatenate([kv[...].reshape(PAGE * half, HEAD_DIM) for kv in kv_refs], axis=0).astype(BF16)
    s = _dot_nt(qs_ref[...].astype(BF16), kvb)
    local_key = lax.broadcasted_iota(jnp.int32, (1, cols), 1) // half
    picked = _dot(sel_ref[0, 0], _block_expander(span // SLC_BLOCK, local_key))
    bias = (picked * own_ref[...] - 1.0) * (-NEG_BIG)
    bias = jnp.concatenate([bias[g * t:(g + 1) * t] for g in range(g_n) for _ in range(r)], axis=0)
    _softmax_step(s, None, kvb, m_ref, l_ref, acc_ref, 0, value_shift=g_n, bias=bias)

    @pl.when(j == n_steps - 1)
    def _():
        tcol = jnp.concatenate([lax.broadcasted_iota(jnp.int32, (t, 1), 0)] * r, axis=0)
        pad = jnp.zeros((PAGE - t, HEAD_DIM), F32)

        def new_keys(ref, g):
            return jnp.concatenate([ref[0][:, g * HEAD_DIM:(g + 1) * HEAD_DIM], pad], axis=0)

        lane = lax.broadcasted_iota(jnp.int32, (rg, PAGE), 1)
        for g in range(g_n):
            qg = qs_ref[g * rg:(g + 1) * rg, :].astype(BF16)
            _softmax_step(_dot_nt(qg, new_keys(ksn_ref, g).astype(BF16)), lane <= tcol,
                          new_keys(vsn_ref, g).astype(BF16), m_ref, l_ref, acc_ref, g * rg)
        o_slc = _softmax_finish(m_ref, l_ref, acc_ref)

        wb = wst_ref.shape[1]
        kpos = past - wb + lax.broadcasted_iota(jnp.int32, (rg, wb + PAGE), 1)
        qpos = past + tcol
        vis = (kpos <= qpos) & (kpos > qpos - WINDOW)
        wst = wst_ref[0]
        gates = gate_ref[0]
        ocmp = ocmp_ref[0]
        outs = []
        for g in range(g_n):
            qg = qs_ref[g * rg:(g + 1) * rg, :].astype(BF16)
            kw = jnp.concatenate([wst[:, g * HEAD_DIM:(g + 1) * HEAD_DIM], new_keys(kwn_ref, g)], axis=0)
            vw = jnp.concatenate([wst[:, NSA_KVW + g * HEAD_DIM:NSA_KVW + (g + 1) * HEAD_DIM],
                                  new_keys(vwn_ref, g)], axis=0)
            o_win = _softmax_once(_dot_nt(qg, kw.astype(BF16)), vis, vw.astype(BF16))
            gt = jax.nn.sigmoid(gates[:, g * V7X_LANES:(g + 1) * V7X_LANES])
            for h in range(r):
                rows = slice(g * rg + h * t, g * rg + (h + 1) * t)
                outs.append(gt[:, 3 * h:3 * h + 1] * ocmp[rows] + gt[:, 3 * h + 1:3 * h + 2] * o_slc[rows]
                            + gt[:, 3 * h + 2:3 * h + 3] * o_win[h * t:(h + 1) * t])
        tok_ref[0] = jnp.concatenate(outs, axis=-1)


def _nsa_sample_attend(cache, pages, proj, sel, ocmp, win_state, win_index0, past, pps):
    b, t, _ = proj.shape
    npg = pages.shape[0] // b
    nb = sel.shape[2]
    wb = win_state.shape[1]
    n_steps = npg // pps
    assert nb * SLC_BLOCK == npg * PAGE == past and t <= SLC_BLOCK and npg % pps == 0
    col = lambda first: first * V7X_LANES // NSA_KVW
    new_spec = lambda name: pl.BlockSpec((1, t, NSA_KVW), lambda bi, j, pg: (bi, 0, col(NSA_COLS[name])))
    page_spec = lambda k: pl.BlockSpec((PAGE, KV_PLANES // 2, HEAD_DIM),
                                       lambda bi, j, pg: (pg[bi * npg + j * pps + k], 1, 0))
    rows = NSA_HEADS * t
    bps = pps * PAGE // SLC_BLOCK
    sel = sel.reshape(b, NSA_KV * t, n_steps, bps).transpose(0, 2, 1, 3)
    cols = pps * PAGE * (KV_PLANES // 2)
    vmem = (2 * (pps * PAGE * NSA_KVW * 2 * 4 + wb * 2 * NSA_KVW * 4 + t * NSA_PROJ_W * 4)
            + 6 * pps * PAGE * HEAD_DIM * 4 + 8 * (wb + PAGE) * 128 * 4 + 8 * NSA_KV * t * cols * 4
            + 6 * rows * cols * 4)
    grid_spec = pltpu.PrefetchScalarGridSpec(
        num_scalar_prefetch=1,
        grid=(b, n_steps),
        in_specs=[page_spec(k) for k in range(pps)] + [
            pl.BlockSpec((1, t, NSA_Q_W), lambda bi, j, pg: (bi, 0, 0)),
            pl.BlockSpec((1, 1, NSA_KV * t, bps), lambda bi, j, pg: (bi, j, 0, 0)),
            pl.BlockSpec((1, rows, HEAD_DIM), lambda bi, j, pg: (bi, 0, 0)),
            new_spec("ks"), new_spec("vs"), new_spec("kw"), new_spec("vw"),
            pl.BlockSpec((1, wb, 2 * NSA_KVW), lambda bi, j, pg: (win_index0 + bi, 0, 0)),
            pl.BlockSpec((1, t, NSA_KV * V7X_LANES), lambda bi, j, pg: (bi, 0, col(NSA_COLS["gates"]))),
        ],
        out_specs=pl.BlockSpec((1, t, NSA_Q_W), lambda bi, j, pg: (bi, 0, 0)),
        scratch_shapes=[pltpu.VMEM((rows, HEAD_DIM), F32), pltpu.VMEM((NSA_KV * t, cols), F32),
                        pltpu.VMEM((rows, V7X_LANES), F32), pltpu.VMEM((rows, V7X_LANES), F32),
                        pltpu.VMEM((rows, HEAD_DIM), F32)],
    )
    return pl.pallas_call(
        functools.partial(_nsa_sample_attend_kernel, past=past, n_steps=n_steps, pps=pps),
        grid_spec=grid_spec,
        out_shape=jax.ShapeDtypeStruct((b, t, NSA_Q_W), F32),
        compiler_params=_cparams(2, vmem),
        name="nsa_sample_attend",
    )(pages, *([cache] * pps), proj, sel, ocmp, proj, proj, proj, proj, win_state, proj)


def _nsa_sample(proj, cache, page_ids, kc, vc, win_state, win_index0, past, pps):
    b, t, _ = proj.shape
    ocmp, pslc = _nsa_sample_cmp(proj, kc, vc, past)
    nb = pslc.shape[2]
    qpos = jnp.broadcast_to(past + jnp.arange(t, dtype=jnp.int32), (b * NSA_KV, t)).reshape(1, -1)
    sel_t = _rank_blocks(pslc.reshape(b * NSA_KV * t, nb).T, qpos, TOP_N - 1)
    sel = sel_t.T.reshape(b, NSA_KV * t, nb).astype(BF16)
    return _nsa_sample_attend(cache, page_ids.reshape(-1), proj, sel, ocmp, win_state, win_index0, past, pps)


PROJ_TM = 1024
PROJ_TN = 512
WO_TM = 512
FFN_TM = 512
FFN_TF = 512
FFN_TN = 512
WO_TN = 1024
MEM_TQ = 512
NSA_TQ = 256
NSA_TK = 1024
CMP_PAGES = 16
SAMPLE_PAGES = 8


def kernel(x_prompt, x_sample, cache_nsa_kv, state_nsa_win, state_ret, state_ffn_conv, cache_mem_kv, page_table,
           mem_prompt, norm1_g, nsa_w_in, nsa_cmp_pe, nsa_cmp_w1, nsa_cmp_w2, ret_w_in, ret_gn_g, mem_norm_g,
           w_mem_kv, w_o, norm2_g, ffn_w_in, ffn_conv_w, ffn_conv_b, ffn_w_out, final_norm_g):
    bp, s_len, d = x_prompt.shape
    db, t_len, _ = x_sample.shape
    depth = norm1_g.shape[0]
    n_phys = cache_nsa_kv.shape[1]
    n_mem = mem_prompt.shape[1]
    fdim = ffn_w_out.shape[1]
    ret_heads = state_ret.shape[2]
    tok_w = ret_heads * RET_HEAD_DIM
    past = page_table.shape[1] * PAGE
    wb = state_nsa_win.shape[2]
    keep_p = min(WINDOW, s_len)
    assert cache_nsa_kv.shape[2] == PAGE and tok_w == NSA_Q_W and wb == WINDOW
    assert s_len % RET_CHUNK == 0 and t_len <= RET_CHUNK and t_len % 8 == 0

    xp = x_prompt.reshape(bp * s_len, d)
    xs = x_sample.reshape(db * t_len, d)
    cache = cache_nsa_kv.reshape(-1, KV_PLANES, HEAD_DIM)
    win_state = state_nsa_win.reshape(-1, wb, 2 * NSA_KVW)
    mem2 = mem_prompt.reshape(bp * n_mem, d)
    prompt_pages = jnp.arange(bp * s_len // PAGE, dtype=jnp.int32)
    pos_p = jnp.arange(s_len, dtype=jnp.int32)
    pos_s = past + jnp.minimum(jnp.arange(RET_CHUNK, dtype=jnp.int32), t_len - 1)
    ret_tab_p = _retention_tables(ret_heads, RET_CHUNK, RET_CHUNK, pos_p)
    ret_tab_s = _retention_tables(ret_heads, RET_CHUNK, t_len, pos_s)
    zero_state = jnp.zeros((bp, ret_heads, RET_HEAD_DIM, RET_HEAD_DIM), F32)
    zero_conv = jnp.zeros((bp, CONV_W - 1, fdim), F32)

    kv_p, kv_s, win_p, win_s, ret_p, ret_s, conv_p, conv_s, mem_p = ([] for _ in range(9))
    for i in range(depth):
        mem_kv_p = _norm_matmul(mem2, mem_norm_g[i], w_mem_kv[i].astype(BF16), n_mem, PROJ_TN)
        mem_kv_p = mem_kv_p.reshape(bp, n_mem, 2 * MEM_W)
        mem_p.append(mem_kv_p.reshape(bp, n_mem, 2, MEM_HEADS, HEAD_DIM))
        mem_kv_s = cache_mem_kv[i].reshape(db, n_mem, 2 * MEM_W)
        if i % 2 == 0:
            a = i // 2
            w_tail = _prep_nsa_w(nsa_w_in[a])
            pe, w1, w2 = nsa_cmp_pe[a], nsa_cmp_w1[a], nsa_cmp_w2[a]
            n_main = NSA_MAIN_W // PROJ_TN
            proj_p = _norm_matmul(xp, norm1_g[i], nsa_w_in, PROJ_TM, PROJ_TN, a, n_main, w_tail)
            proj_s = _norm_matmul(xs, norm1_g[i], nsa_w_in, db * t_len, PROJ_TN, a, n_main, w_tail)
            p3 = proj_p.reshape(bp, s_len, NSA_PROJ_W)
            s3 = proj_s.reshape(db, t_len, NSA_PROJ_W)
            rows_lo, rows_hi = NSA_Q_W, NSA_Q_W + NSA_ROWS_W
            kv_p.append(p3[:, :, rows_lo:rows_hi].reshape(bp, s_len, 4, NSA_KV, HEAD_DIM))
            kv_s.append(s3[:, :, rows_lo:rows_hi].reshape(db, t_len, 4, NSA_KV, HEAD_DIM))
            cmp = _even_odd(_compress(proj_p, prompt_pages, pe, w1, w2, CMP_PAGES, col0=NSA_Q_W), bp,
                            planes_first=True)
            tok_p = _nsa_prompt(p3, cmp[0], cmp[1], NSA_COLS, NSA_TQ, NSA_TK)
            page_ids = page_table + a * n_phys
            cmp = _even_odd(_compress(cache, page_ids.reshape(-1), pe, w1, w2, CMP_PAGES), db)
            tok_s = _nsa_sample(s3, cache, page_ids, cmp[0], cmp[1], win_state, a * db, past, SAMPLE_PAGES)
            qm_block = NSA_MAIN_W // MEM_W
            win_p.append(p3[:, s_len - keep_p:, rows_hi:NSA_MAIN_W].reshape(bp, keep_p, 2, NSA_KV, HEAD_DIM))
            new_win = s3[:, :, rows_hi:NSA_MAIN_W].reshape(db, t_len, 2, NSA_KV, HEAD_DIM)
            win_s.append(jnp.concatenate([state_nsa_win[a], new_win], axis=1)[:, -wb:])
        else:
            bl = i // 2
            w_in = ret_w_in[bl].astype(BF16)
            proj_p = _norm_matmul(xp, norm1_g[i], w_in, PROJ_TM, PROJ_TN)
            proj_s = _norm_matmul(xs, norm1_g[i], w_in, db * t_len, PROJ_TN)
            p3 = proj_p.reshape(bp, s_len, -1)
            s3 = proj_s.reshape(db, t_len, -1)
            tok_p, sp = _retention(p3, ret_gn_g[bl], zero_state, *ret_tab_p)
            tok_s, ss = _retention(s3, ret_gn_g[bl], state_ret[bl], *ret_tab_s)
            ret_p.append(sp)
            ret_s.append(ss)
            qm_block = 4 * tok_w // MEM_W
        mem_out_p = _mem_attend(p3, qm_block, mem_kv_p, MEM_TQ)
        mem_out_s = _mem_attend(s3, qm_block, mem_kv_s, t_len)
        wo = w_o[i].astype(BF16)
        xp = _wo(xp, tok_p.reshape(bp * s_len, tok_w), mem_out_p.reshape(bp * s_len, MEM_W), wo[:tok_w], wo[tok_w:],
                 WO_TM, WO_TN)
        xs = _wo(xs, tok_s.reshape(db * t_len, tok_w), mem_out_s.reshape(db * t_len, MEM_W), wo[:tok_w], wo[tok_w:],
                 db * t_len, WO_TN)
        last = i == depth - 1
        ffn_w = (norm2_g[i], ffn_w_in[i].astype(BF16), ffn_conv_w[i], ffn_conv_b[i], ffn_w_out[i].astype(BF16))
        xp, tails = _ffn(xp, *ffn_w, zero_conv, final_norm_g, seq_len=s_len, tm=FFN_TM, tf=FFN_TF, tn=FFN_TN,
                         final_norm=last)
        conv_p.append(tails[s_len // FFN_TM - 1::s_len // FFN_TM])
        xs, up = _ffn(xs, *ffn_w, state_ffn_conv[i], final_norm_g, seq_len=t_len, tm=db * t_len, tf=FFN_TF,
                      tn=FFN_TN, final_norm=last)
        conv_s.append(up.reshape(db, t_len, fdim)[:, t_len - (CONV_W - 1):])
    return (xp.reshape(bp, s_len, d), xs.reshape(db, t_len, d), jnp.stack(kv_p), jnp.stack(kv_s), jnp.stack(win_p),
            jnp.stack(win_s), jnp.stack(ret_p), jnp.stack(ret_s), jnp.stack(conv_p), jnp.stack(conv_s),
            jnp.stack(mem_p))
```
